```python
import jax, jax.numpy as jnp
from jax import lax
import numpy as np

D_MODEL = 1024
BATCH = 8
SEQ = 2048
DEPTH = 4
DEC_BATCH = 32
DEC_SEQ = 2048
PAST_LEN = 128

N_MIXERS = 2
N_POOL_LAYERS = (DEPTH + 1) // 2
N_MLA_LAYERS = DEPTH // 2
POOL_WINDOWS = (2, 4, 8, 16)
POOL_GROUP = D_MODEL // len(POOL_WINDOWS)
N_HEADS = 16
QK_NOPE = 64
QK_ROPE = 32
V_HEAD = 64
Q_LORA = 256
KV_LORA = 128
ROPE_THETA = 10000.0
Q_BLOCK = 128
N_GROUPS = 8
EXPERTS_PER_GROUP = 8
N_EXPERTS = N_GROUPS * EXPERTS_PER_GROUP
TOP_K = 2
D_EXPERT = 512
MOE_BLOCK = 128
EPS = 1e-6

kernel_name = "hybrid_pool_mla_hmoe_encoder"


def rmsnorm(x, g):
    xf = x.astype(jnp.float32)
    y = xf * lax.rsqrt(jnp.mean(xf * xf, axis=-1, keepdims=True) + EPS)
    return (y * g.astype(jnp.float32)).astype(x.dtype)


def rope_tables(seq_len):
    inv_freq = 1.0 / (ROPE_THETA ** (jnp.arange(0, QK_ROPE, 2, dtype=jnp.float32) / QK_ROPE))
    ang = jnp.arange(seq_len, dtype=jnp.float32)[:, None] * inv_freq[None, :]
    return jnp.cos(ang), jnp.sin(ang)


def apply_rope(x, cos, sin):
    half = QK_ROPE // 2
    xf = x.astype(jnp.float32)
    x1, x2 = xf[..., :half], xf[..., half:]
    return jnp.concatenate([x1 * cos - x2 * sin, x2 * cos + x1 * sin], axis=-1).astype(x.dtype)


def pool_mixer(h, w_pool, pool_scale):
    B, S, D = h.shape
    hf = h.astype(jnp.float32)
    csum = jnp.concatenate([jnp.zeros((B, 1, D), jnp.float32), jnp.cumsum(hf, axis=1)], axis=1)
    pos = jnp.arange(S)
    parts = []
    for g, w in enumerate(POOL_WINDOWS):
        lo = jnp.clip(pos - w // 2, 0, S)
        hi = jnp.clip(pos + (w - 1 - w // 2) + 1, 0, S)
        cg = csum[:, :, g * POOL_GROUP:(g + 1) * POOL_GROUP]
        cnt = (hi - lo).astype(jnp.float32)
        parts.append((cg[:, hi] - cg[:, lo]) / cnt[None, :, None])
    pooled = jnp.concatenate(parts, axis=-1) - hf
    mixed = jnp.einsum('bsgc,gcd->bsgd', pooled.reshape(B, S, len(POOL_WINDOWS), POOL_GROUP),
                       w_pool.astype(jnp.float32)).reshape(B, S, D)
    return (mixed * pool_scale.astype(jnp.float32)).astype(h.dtype)


def mla(h, cos, sin, w_dq, q_norm, w_uq, w_dkv, kv_norm, w_ukv, w_o):
    B, S, _ = h.shape
    cq = rmsnorm(h @ w_dq, q_norm)
    q = (cq @ w_uq).reshape(B, S, N_HEADS, QK_NOPE + QK_ROPE)
    qn = q[..., :QK_NOPE]
    qr = apply_rope(q[..., QK_NOPE:], cos[None, :, None, :], sin[None, :, None, :])
    ckv_kr = h @ w_dkv
    ckv = rmsnorm(ckv_kr[..., :KV_LORA], kv_norm)
    kr = apply_rope(ckv_kr[..., KV_LORA:], cos[None], sin[None])
    kv = (ckv @ w_ukv).reshape(B, S, N_HEADS, QK_NOPE + V_HEAD)
    kn, v = kv[..., :QK_NOPE], kv[..., QK_NOPE:]
    scale = (QK_NOPE + QK_ROPE) ** -0.5
    nb = S // Q_BLOCK

    def to_blocks(t):
        return t.reshape(B, nb, Q_BLOCK, *t.shape[2:]).swapaxes(0, 1)

    def attend(blk):
        qn_b, qr_b = blk
        s = (jnp.einsum('bqhn,bkhn->bhqk', qn_b, kn, preferred_element_type=jnp.float32)
             + jnp.einsum('bqhr,bkr->bhqk', qr_b, kr, preferred_element_type=jnp.float32)) * scale
        p = jax.nn.softmax(s, axis=-1).astype(v.dtype)
        return jnp.einsum('bhqk,bkhv->bqhv', p, v)

    o = lax.map(attend, (to_blocks(qn), to_blocks(qr)))
    o = o.swapaxes(0, 1).reshape(B, S, N_HEADS * V_HEAD)
    return o @ w_o


def hier_moe(h, w_group, b_group, w_router, b_router, w_gate, w_up, w_down):
    B, S, D = h.shape
    T = B * S
    xt = h.reshape(T, D)
    gp = jax.nn.softmax((xt @ w_group).astype(jnp.float32) + b_group.astype(jnp.float32), axis=-1)
    g_p, g_idx = lax.top_k(gp, 1)
    elog = ((xt @ w_router).astype(jnp.float32) + b_router.astype(jnp.float32)).reshape(T, N_GROUPS, EXPERTS_PER_GROUP)
    elog_sel = jnp.take_along_axis(elog, g_idx[:, :, None], axis=1)[:, 0]
    e_p, e_idx = lax.top_k(jax.nn.softmax(elog_sel, axis=-1), TOP_K)
    e_p = e_p / jnp.sum(e_p, axis=-1, keepdims=True)
    weights = g_p * e_p
    expert_ids = g_idx * EXPERTS_PER_GROUP + e_idx

    A = T * TOP_K
    NB = -(-A // MOE_BLOCK) + N_EXPERTS
    P = NB * MOE_BLOCK
    flat_e = expert_ids.reshape(A).astype(jnp.int32)
    flat_tok = jnp.repeat(jnp.arange(T, dtype=jnp.int32), TOP_K)
    flat_w = weights.reshape(A)
    order = jnp.argsort(flat_e)
    sorted_e = flat_e[order]
    counts = jnp.bincount(flat_e, length=N_EXPERTS)
    starts = jnp.cumsum(counts) - counts
    padded_counts = ((counts + MOE_BLOCK - 1) // MOE_BLOCK) * MOE_BLOCK
    padded_ends = jnp.cumsum(padded_counts)
    padded_starts = padded_ends - padded_counts
    dest = padded_starts[sorted_e] + (jnp.arange(A) - starts[sorted_e])
    buf_tok = jnp.full((P,), T, jnp.int32).at[dest].set(flat_tok[order])
    buf_w = jnp.zeros((P,), jnp.float32).at[dest].set(flat_w[order])
    block_start = jnp.arange(NB) * MOE_BLOCK
    block_expert = jnp.minimum(jnp.sum(padded_ends[None, :] <= block_start[:, None], axis=1), N_EXPERTS - 1)

    x_pad = jnp.concatenate([xt, jnp.zeros((1, D), xt.dtype)], axis=0)
    xs = x_pad[buf_tok].reshape(NB, MOE_BLOCK, D)

    def expert_block(args):
        xb, e = args
        hid = jax.nn.silu(xb @ w_gate[e]) * (xb @ w_up[e])
        return hid @ w_down[e]

    ys = lax.map(expert_block, (xs, block_expert)).reshape(P, D)
    y = jnp.zeros((T + 1, D), jnp.float32).at[buf_tok].add(ys.astype(jnp.float32) * buf_w[:, None])[:T]
    return y.reshape(B, S, D).astype(h.dtype)


def trunk(x, norm_mix, norm_ffn, norm_final, pool_w, pool_scale,
          mla_w_dq, mla_q_norm, mla_w_uq, mla_w_dkv, mla_kv_norm, mla_w_ukv, mla_w_o,
          moe_w_group, moe_b_group, moe_w_router, moe_b_router, moe_w_gate, moe_w_up, moe_w_down):
    cos, sin = rope_tables(x.shape[1])
    for i in range(DEPTH):
        h = rmsnorm(x, norm_mix[i])
        j = i // N_MIXERS
        if i % N_MIXERS == 0:
            x = x + pool_mixer(h, pool_w[j], pool_scale[j])
        else:
            x = x + mla(h, cos, sin, mla_w_dq[j], mla_q_norm[j], mla_w_uq[j], mla_w_dkv[j],
                        mla_kv_norm[j], mla_w_ukv[j], mla_w_o[j])
        h = rmsnorm(x, norm_ffn[i])
        x = x + hier_moe(h, moe_w_group[i], moe_b_group[i], moe_w_router[i], moe_b_router[i],
                         moe_w_gate[i], moe_w_up[i], moe_w_down[i])
    return rmsnorm(x, norm_final)


def setup_inputs(seed: int = 0) -> dict:
    key = jax.random.key(seed)
    ks = jax.random.split(key, 24)
    f32 = jnp.float32

    def nrm(k, shape, fan_in):
        return jax.random.normal(k, shape, f32) * (fan_in ** -0.5)

    def gain(k, shape):
        return 1.0 + 0.1 * jax.random.normal(k, shape, f32)

    return {
        "x_prompt": jax.random.normal(ks[0], (BATCH, SEQ, D_MODEL), f32),
        "x_sample": jax.random.normal(ks[1], (DEC_BATCH, DEC_SEQ, D_MODEL), f32),
        "norm_mix": gain(ks[2], (DEPTH, D_MODEL)),
        "norm_ffn": gain(ks[3], (DEPTH, D_MODEL)),
        "norm_final": gain(ks[4], (D_MODEL,)),
        "pool_w": nrm(ks[5], (N_POOL_LAYERS, len(POOL_WINDOWS), POOL_GROUP, POOL_GROUP), POOL_GROUP),
        "pool_scale": gain(ks[6], (N_POOL_LAYERS, D_MODEL)),
        "mla_w_dq": nrm(ks[7], (N_MLA_LAYERS, D_MODEL, Q_LORA), D_MODEL),
        "mla_q_norm": gain(ks[8], (N_MLA_LAYERS, Q_LORA)),
        "mla_w_uq": nrm(ks[9], (N_MLA_LAYERS, Q_LORA, N_HEADS * (QK_NOPE + QK_ROPE)), Q_LORA),
        "mla_w_dkv": nrm(ks[10], (N_MLA_LAYERS, D_MODEL, KV_LORA + QK_ROPE), D_MODEL),
        "mla_kv_norm": gain(ks[11], (N_MLA_LAYERS, KV_LORA)),
        "mla_w_ukv": nrm(ks[12], (N_MLA_LAYERS, KV_LORA, N_HEADS * (QK_NOPE + V_HEAD)), KV_LORA),
        "mla_w_o": nrm(ks[13], (N_MLA_LAYERS, N_HEADS * V_HEAD, D_MODEL), N_HEADS * V_HEAD),
        "moe_w_group": nrm(ks[14], (DEPTH, D_MODEL, N_GROUPS), D_MODEL),
        "moe_b_group": 0.01 * jax.random.normal(ks[15], (DEPTH, N_GROUPS), f32),
        "moe_w_router": nrm(ks[16], (DEPTH, D_MODEL, N_EXPERTS), D_MODEL),
        "moe_b_router": 0.01 * jax.random.normal(ks[17], (DEPTH, N_EXPERTS), f32),
        "moe_w_gate": nrm(ks[18], (DEPTH, N_EXPERTS, D_MODEL, D_EXPERT), D_MODEL),
        "moe_w_up": nrm(ks[19], (DEPTH, N_EXPERTS, D_MODEL, D_EXPERT), D_MODEL),
        "moe_w_down": nrm(ks[20], (DEPTH, N_EXPERTS, D_EXPERT, D_MODEL), D_EXPERT),
    }


def reference(x_prompt, x_sample, norm_mix, norm_ffn, norm_final, pool_w, pool_scale,
              mla_w_dq, mla_q_norm, mla_w_uq, mla_w_dkv, mla_kv_norm, mla_w_ukv, mla_w_o,
              moe_w_group, moe_b_group, moe_w_router, moe_b_router, moe_w_gate, moe_w_up, moe_w_down):
    y_prompt = trunk(x_prompt, norm_mix, norm_ffn, norm_final, pool_w, pool_scale,
                     mla_w_dq, mla_q_norm, mla_w_uq, mla_w_dkv, mla_kv_norm, mla_w_ukv, mla_w_o,
                     moe_w_group, moe_b_group, moe_w_router, moe_b_router, moe_w_gate, moe_w_up, moe_w_down)
    y_sample = trunk(x_sample, norm_mix, norm_ffn, norm_final, pool_w, pool_scale,
                     mla_w_dq, mla_q_norm, mla_w_uq, mla_w_dkv, mla_kv_norm, mla_w_ukv, mla_w_o,
                     moe_w_group, moe_b_group, moe_w_router, moe_b_router, moe_w_gate, moe_w_up, moe_w_down)
    return (y_prompt, y_sample)
```

```python
import functools
import math

import jax
import jax.numpy as jnp
from jax import lax
from jax.experimental import pallas as pl
from jax.experimental.pallas import tpu as pltpu

F32 = jnp.float32
BF16 = jnp.bfloat16

D_MODEL = 1024
DEPTH = 4
POOL_WINDOWS = (2, 4, 8, 16)
POOL_GROUP = D_MODEL // len(POOL_WINDOWS)
POOL_HALO = 8
N_HEADS = 16
QK_NOPE = 64
QK_ROPE = 32
V_HEAD = 64
Q_LORA = 256
KV_LORA = 128
ROPE_THETA = 10000.0
N_GROUPS = 8
EXPERTS_PER_GROUP = 8
N_EXPERTS = N_GROUPS * EXPERTS_PER_GROUP
D_EXPERT = 512
EPS = 1e-6

LANES = 128
KEY_WIDTH = 2 * LANES
SEQ_TILE = 256
ROW_TILE = 512
MOE_BLOCK = 256
VMEM_LIMIT = 48 * 1024 * 1024
SCORE_SCALE = (QK_NOPE + QK_ROPE) ** -0.5 * math.log2(math.e)

L_E1, L_E2, L_W1, L_W2, L_R1, L_R2 = range(6)


def _rms(v, axis=-1):
    return v * lax.rsqrt(jnp.mean(v * v, axis=axis, keepdims=True) + EPS)


def _ffn_norm_and_route(x1, gffn_ref, wr_ref, br_ref, tri_ref, h2_ref, slab_ref, cnt_ref):
    h2 = _rms(x1) * gffn_ref[...]
    h2_ref[...] = h2
    logits = jnp.dot(h2.astype(BF16), wr_ref[...], preferred_element_type=F32) + br_ref[...]
    lane = lax.broadcasted_iota(jnp.int32, logits.shape, 1)
    lanef = lane.astype(F32)
    neg = jnp.float32(-jnp.inf)
    big = jnp.float32(1e9)

    gmask = (lane >= N_EXPERTS) & (lane < N_EXPERTS + N_GROUPS)
    gl = jnp.where(gmask, logits, neg)
    gmax = jnp.max(gl, axis=-1, keepdims=True)
    gsum = jnp.sum(jnp.exp(gl - gmax), axis=-1, keepdims=True)
    g_p = 1.0 / gsum
    gidx = jnp.min(jnp.where(gl == gmax, lanef, big), axis=-1, keepdims=True) - N_EXPERTS
    lo = gidx * EXPERTS_PER_GROUP
    emask = (lanef >= lo) & (lanef < lo + EXPERTS_PER_GROUP)
    el = jnp.where(emask, logits, neg)
    m1 = jnp.max(el, axis=-1, keepdims=True)
    i1 = jnp.min(jnp.where(el == m1, lanef, big), axis=-1, keepdims=True)
    el2 = jnp.where(lanef == i1, neg, el)
    m2 = jnp.max(el2, axis=-1, keepdims=True)
    i2 = jnp.min(jnp.where(el2 == m2, lanef, big), axis=-1, keepdims=True)
    t = jnp.exp(m2 - m1)
    w1 = g_p / (1.0 + t)
    w2 = w1 * t

    oh1 = jnp.where(lanef == i1, 1.0, 0.0)
    oh2 = jnp.where(lanef == i2, 1.0, 0.0)
    oh = oh1 + oh2
    before = jnp.dot(tri_ref[...], oh.astype(BF16), preferred_element_type=F32) + cnt_ref[...]
    r1 = jnp.sum(oh1 * before, axis=-1, keepdims=True)
    r2 = jnp.sum(oh2 * before, axis=-1, keepdims=True)
    cnt_ref[...] = cnt_ref[...] + jnp.sum(oh, axis=0, keepdims=True)

    slab = jnp.zeros_like(logits)
    for l, v in ((L_E1, i1), (L_E2, i2), (L_W1, w1), (L_W2, w2), (L_R1, r1), (L_R2, r2)):
        slab = jnp.where(lane == l, v, slab)
    slab_ref[...] = slab


def _first_step():
    return (pl.program_id(0) == 0) & (pl.program_id(1) == 0)


def _pool_kernel(seq_len, x_ref, xp_ref, xn_ref, gmix_ref, pw_ref, ps_ref, gffn_ref, wr_ref, br_ref,
                 tri_ref, x1_ref, h2_ref, slab_ref, cnt_ref, hcat_ref):
    i = pl.program_id(1)
    ts = x_ref.shape[0]

    @pl.when(_first_step())
    def _():
        cnt_ref[...] = jnp.zeros_like(cnt_ref)

    g = gmix_ref[...]
    x = x_ref[...]
    h = _rms(x) * g
    hp = jnp.where(i > 0, _rms(xp_ref[...]) * g, 0.0)
    hn = jnp.where(i < pl.num_programs(1) - 1, _rms(xn_ref[...]) * g, 0.0)
    hcat_ref[0:POOL_HALO, :] = hp
    hcat_ref[POOL_HALO:POOL_HALO + ts, :] = h
    hcat_ref[POOL_HALO + ts:2 * POOL_HALO + ts, :] = hn

    pos = i * ts + lax.broadcasted_iota(jnp.int32, (ts, 1), 0)
    for gi, w in enumerate(POOL_WINDOWS):
        c0 = gi * POOL_GROUP
        acc = None
        for j in range(-(w // 2), w // 2):
            v = hcat_ref[POOL_HALO + j:POOL_HALO + j + ts, c0:c0 + POOL_GROUP]
            acc = v if acc is None else acc + v
        cnt = (jnp.minimum(pos + w // 2, seq_len) - jnp.maximum(pos - w // 2, 0)).astype(F32)
        pooled = acc / cnt - h[:, c0:c0 + POOL_GROUP]
        mixed = jnp.dot(pooled.astype(BF16), pw_ref[gi], preferred_element_type=F32)
        x1_ref[:, c0:c0 + POOL_GROUP] = x[:, c0:c0 + POOL_GROUP] + mixed * ps_ref[:, c0:c0 + POOL_GROUP]

    _ffn_norm_and_route(x1_ref[...], gffn_ref, wr_ref, br_ref, tri_ref, h2_ref, slab_ref, cnt_ref)


def _tail_out(batch, seq, ts):
    tok = lambda b, i: (b * (seq // ts) + i, 0)
    shapes = (jax.ShapeDtypeStruct((batch * seq, D_MODEL), F32),
              jax.ShapeDtypeStruct((batch * seq, D_MODEL), F32),
              jax.ShapeDtypeStruct((batch * seq, LANES), F32),
              jax.ShapeDtypeStruct((1, LANES), F32))
    specs = (pl.BlockSpec((ts, D_MODEL), tok), pl.BlockSpec((ts, D_MODEL), tok),
             pl.BlockSpec((ts, LANES), tok), pl.BlockSpec((1, LANES), lambda b, i: (0, 0)))
    return shapes, specs


def _const_spec(shape):
    nd = len(shape)
    return pl.BlockSpec(shape, lambda b, i: (0,) * nd)


def _pool_layer(x, gmix, pw, ps, gffn, wr, br, tri):
    batch, seq, _ = x.shape
    ts = min(SEQ_TILE, seq)
    hb = ts // POOL_HALO
    nh = seq // POOL_HALO
    out_shapes, out_specs = _tail_out(batch, seq, ts)
    return pl.pallas_call(
        functools.partial(_pool_kernel, seq),
        grid=(batch, seq // ts),
        in_specs=[
            pl.BlockSpec((None, ts, D_MODEL), lambda b, i: (b, i, 0)),
            pl.BlockSpec((None, POOL_HALO, D_MODEL), lambda b, i: (b, jnp.maximum(i * hb - 1, 0), 0)),
            pl.BlockSpec((None, POOL_HALO, D_MODEL), lambda b, i: (b, jnp.minimum((i + 1) * hb, nh - 1), 0)),
            _const_spec((1, D_MODEL)),
            _const_spec((len(POOL_WINDOWS), POOL_GROUP, POOL_GROUP)),
            _const_spec((1, D_MODEL)),
            _const_spec((1, D_MODEL)),
            _const_spec((D_MODEL, LANES)),
            _const_spec((1, LANES)),
            _const_spec((ts, ts)),
        ],
        out_specs=out_specs,
        out_shape=out_shapes,
        scratch_shapes=[pltpu.VMEM((ts + 2 * POOL_HALO, D_MODEL), F32)],
        compiler_params=pltpu.CompilerParams(
            dimension_semantics=("arbitrary", "arbitrary"), vmem_limit_bytes=VMEM_LIMIT),
        name="pool_mixer",
    )(x, x, x, gmix, pw, ps, gffn, wr, br, tri)


def _absorb_kernel(wk_ref, wqt_ref, out_ref):
    out_ref[...] = jnp.dot(wk_ref[...], wqt_ref[...], preferred_element_type=F32,
                           precision=lax.Precision.HIGHEST).astype(out_ref.dtype)


def _absorb_weights(wk, wqt):
    return pl.pallas_call(
        _absorb_kernel,
        grid=(N_HEADS,),
        in_specs=[pl.BlockSpec((None, KV_LORA, QK_NOPE), lambda h: (h, 0, 0)),
                  pl.BlockSpec((None, QK_NOPE, Q_LORA), lambda h: (h, 0, 0))],
        out_specs=pl.BlockSpec((None, KV_LORA, Q_LORA), lambda h: (h, 0, 0)),
        out_shape=jax.ShapeDtypeStruct((N_HEADS, KV_LORA, Q_LORA), BF16),
        name="absorb_weights",
    )(wk, wqt)


_NT = (((1,), (1,)), ((), ()))
_TN = (((0,), (0,)), ((), ()))


def _mla_proj_kernel(x_ref, gmix_ref, wdqt_ref, qn_ref, wkv_ref, wckvt_ref, kvn_row_ref, kvn_col_ref,
                     cos2_ref, sin2_ref, cqt_ref, ksh_ref, ckvt_ref):
    hb = (_rms(x_ref[...]) * gmix_ref[...]).astype(BF16)
    cqt = lax.dot_general(wdqt_ref[...], hb, _NT, preferred_element_type=F32)
    cqt_ref[...] = (_rms(cqt, axis=0) * qn_ref[...]).astype(BF16)
    kv = jnp.dot(hb, wkv_ref[...], preferred_element_type=F32)
    ckv = _rms(kv[:, 0:KV_LORA]) * kvn_row_ref[...]
    kr = kv[:, LANES:2 * LANES] * cos2_ref[...] + kv[:, 2 * LANES:3 * LANES] * sin2_ref[...]
    ksh_ref[:, 0:LANES] = ckv.astype(BF16)
    ksh_ref[:, LANES:2 * LANES] = kr.astype(BF16)
    ckvt = lax.dot_general(wckvt_ref[...], hb, _NT, preferred_element_type=F32)
    ckvt_ref[...] = (_rms(ckvt, axis=0) * kvn_col_ref[...]).astype(BF16)


def _mla_proj(x, gmix, wdqt, qn_col, wkv, wckvt, kvn_row, kvn_col, cos2, sin2):
    batch, seq, _ = x.shape
    ts = min(SEQ_TILE, seq)
    return pl.pallas_call(
        _mla_proj_kernel,
        grid=(batch, seq // ts),
        in_specs=[
            pl.BlockSpec((None, ts, D_MODEL), lambda b, i: (b, i, 0)),
            _const_spec((1, D_MODEL)),
            _const_spec((Q_LORA, D_MODEL)),
            _const_spec((Q_LORA, 1)),
            _const_spec((D_MODEL, 3 * LANES)),
            _const_spec((KV_LORA, D_MODEL)),
            _const_spec((1, KV_LORA)),
            _const_spec((KV_LORA, 1)),
            pl.BlockSpec((ts, LANES), lambda b, i: (i, 0)),
            pl.BlockSpec((ts, LANES), lambda b, i: (i, 0)),
        ],
        out_specs=(pl.BlockSpec((None, Q_LORA, ts), lambda b, i: (b, 0, i)),
                   pl.BlockSpec((None, ts, KEY_WIDTH), lambda b, i: (b, i, 0)),
                   pl.BlockSpec((None, KV_LORA, ts), lambda b, i: (b, 0, i))),
        out_shape=(jax.ShapeDtypeStruct((batch, Q_LORA, seq), BF16),
                   jax.ShapeDtypeStruct((batch, seq, KEY_WIDTH), BF16),
                   jax.ShapeDtypeStruct((batch, KV_LORA, seq), BF16)),
        compiler_params=pltpu.CompilerParams(
            dimension_semantics=("arbitrary", "arbitrary"), vmem_limit_bytes=VMEM_LIMIT),
        name="mla_proj",
    )(x, gmix, wdqt, qn_col, wkv, wckvt, kvn_row, kvn_col, cos2, sin2)


def _attn_kernel(x_ref, cqt_ref, ksh_ref, ckvt_ref, cost_ref, sint_ref, wabst_ref, wqrt_ref, wvt_ref, wo_ref,
                 gffn_ref, wr_ref, br_ref, tri_ref, x1_ref, h2_ref, slab_ref, cnt_ref, o_scr):
    tq = x_ref.shape[0]
    half = QK_ROPE // 2

    @pl.when(_first_step())
    def _():
        cnt_ref[...] = jnp.zeros_like(cnt_ref)

    cq = cqt_ref[...]
    cost = cost_ref[...]
    sint = sint_ref[...]
    pad = jnp.zeros((KEY_WIDTH - KV_LORA - QK_ROPE, tq), F32)

    def head(h, carry):
        qn = jnp.dot(wabst_ref[h], cq, preferred_element_type=F32)
        qr = jnp.dot(wqrt_ref[h], cq, preferred_element_type=F32)
        a, b = qr[0:half], qr[half:QK_ROPE]
        qp = jnp.concatenate([qn, a * cost - b * sint, b * cost + a * sint, pad], axis=0)
        qp = (qp * SCORE_SCALE).astype(BF16)
        st = jnp.dot(ksh_ref[...], qp, preferred_element_type=F32)
        p = jnp.exp2(st - jnp.max(st, axis=0, keepdims=True))
        denom = jnp.sum(p, axis=0, keepdims=True)
        ot = jnp.dot(ckvt_ref[...], p.astype(BF16), preferred_element_type=F32)
        ot = ot * (1.0 / denom)
        ov = jnp.dot(wvt_ref[h], ot.astype(BF16), preferred_element_type=F32)
        o_scr[pl.ds(pl.multiple_of(h * V_HEAD, V_HEAD), V_HEAD), :] = ov.astype(BF16)
        return carry

    lax.fori_loop(0, N_HEADS, head, 0)
    attn = lax.dot_general(o_scr[...], wo_ref[...], _TN, preferred_element_type=F32)
    x1 = x_ref[...] + attn
    x1_ref[...] = x1
    _ffn_norm_and_route(x1, gffn_ref, wr_ref, br_ref, tri_ref, h2_ref, slab_ref, cnt_ref)


def _attn_layer(x, cqt, ksh, ckvt, cost, sint, wabst, wqrt, wvt, wo, gffn, wr, br, tri):
    batch, seq, _ = x.shape
    tq = min(SEQ_TILE, seq)
    out_shapes, out_specs = _tail_out(batch, seq, tq)
    return pl.pallas_call(
        _attn_kernel,
        grid=(batch, seq // tq),
        in_specs=[
            pl.BlockSpec((None, tq, D_MODEL), lambda b, i: (b, i, 0)),
            pl.BlockSpec((None, Q_LORA, tq), lambda b, i: (b, 0, i)),
            pl.BlockSpec((None, seq, KEY_WIDTH), lambda b, i: (b, 0, 0)),
            pl.BlockSpec((None, KV_LORA, seq), lambda b, i: (b, 0, 0)),
            pl.BlockSpec((QK_ROPE // 2, tq), lambda b, i: (0, i)),
            pl.BlockSpec((QK_ROPE // 2, tq), lambda b, i: (0, i)),
            _const_spec((N_HEADS, KV_LORA, Q_LORA)),
            _const_spec((N_HEADS, QK_ROPE, Q_LORA)),
            _const_spec((N_HEADS, V_HEAD, KV_LORA)),
            _const_spec((N_HEADS * V_HEAD, D_MODEL)),
            _const_spec((1, D_MODEL)),
            _const_spec((D_MODEL, LANES)),
            _const_spec((1, LANES)),
            _const_spec((tq, tq)),
        ],
        out_specs=out_specs,
        out_shape=out_shapes,
        scratch_shapes=[pltpu.VMEM((N_HEADS * V_HEAD, tq), BF16)],
        compiler_params=pltpu.CompilerParams(
            dimension_semantics=("arbitrary", "arbitrary"), vmem_limit_bytes=VMEM_LIMIT),
        name="mla_attention",
    )(x, cqt, ksh, ckvt, cost, sint, wabst, wqrt, wvt, wo, gffn, wr, br, tri)


def _row_copy(src, src_row, dst, dst_row, sem):
    return pltpu.make_async_copy(src.at[pl.ds(src_row, 1), :], dst.at[pl.ds(dst_row, 1), :], sem)


def _scatter_kernel(zb_ref, zv_ref, h2_ref, dest_hbm, xs_hbm, idx_smem, zeros_vmem, sem_idx, sem_zero, sem_rows):
    i = pl.program_id(0)
    ts = h2_ref.shape[0]
    bm = zeros_vmem.shape[0]
    n_zero = zb_ref.shape[0]

    def zero_copy(e):
        return pltpu.make_async_copy(zeros_vmem, xs_hbm.at[pl.ds(pl.multiple_of(zb_ref[e], bm), bm), :], sem_zero)

    @pl.when(i == 0)
    def _():
        zeros_vmem[...] = jnp.zeros_like(zeros_vmem)

        def start(e, c):
            @pl.when(zv_ref[e] > 0)
            def _():
                zero_copy(e).start()
            return c

        def wait(e, c):
            @pl.when(zv_ref[e] > 0)
            def _():
                zero_copy(e).wait()
            return c

        lax.fori_loop(0, n_zero, start, 0)
        lax.fori_loop(0, n_zero, wait, 0)

    idx_copy = pltpu.make_async_copy(dest_hbm.at[i], idx_smem, sem_idx)
    idx_copy.start()
    idx_copy.wait()

    def start(r, c):
        for k in range(2):
            _row_copy(h2_ref, r, xs_hbm, idx_smem[2 * r + k], sem_rows).start()
        return c

    def wait(r, c):
        for k in range(2):
            _row_copy(h2_ref, r, xs_hbm, idx_smem[2 * r + k], sem_rows).wait()
        return c

    lax.fori_loop(0, ts, start, 0)
    lax.fori_loop(0, ts, wait, 0)


def _scatter_rows(h2, dest, zb, zv, n_rows, bm):
    tokens = h2.shape[0]
    ts = dest.shape[1] // 2
    return pl.pallas_call(
        _scatter_kernel,
        grid_spec=pltpu.PrefetchScalarGridSpec(
            num_scalar_prefetch=2,
            grid=(tokens // ts,),
            in_specs=[pl.BlockSpec((ts, D_MODEL), lambda i, zb, zv: (i, 0)),
                      pl.BlockSpec(memory_space=pl.ANY)],
            out_specs=pl.BlockSpec(memory_space=pl.ANY),
            scratch_shapes=[pltpu.SMEM((2 * ts,), jnp.int32),
                            pltpu.VMEM((bm, D_MODEL), F32),
                            pltpu.SemaphoreType.DMA, pltpu.SemaphoreType.DMA, pltpu.SemaphoreType.DMA],
        ),
        out_shape=jax.ShapeDtypeStruct((n_rows, D_MODEL), F32),
        compiler_params=pltpu.CompilerParams(
            dimension_semantics=("arbitrary",), vmem_limit_bytes=VMEM_LIMIT),
        name="moe_scatter",
    )(zb, zv, h2, dest)


def _expert_kernel(be_ref, nv_ref, xs_ref, wg_ref, wu_ref, wd_ref, ys_ref, wg_bf, wu_bf, wd_bf):
    b = pl.program_id(0)

    @pl.when(b < nv_ref[0])
    def _():
        changed = (b == 0) | (be_ref[b] != be_ref[jnp.maximum(b - 1, 0)])

        @pl.when(changed)
        def _():
            wg_bf[...] = wg_ref[...].astype(BF16)
            wu_bf[...] = wu_ref[...].astype(BF16)
            wd_bf[...] = wd_ref[...].astype(BF16)

        xb = xs_ref[...].astype(BF16)
        gate = jnp.dot(xb, wg_bf[...], preferred_element_type=F32)
        up = jnp.dot(xb, wu_bf[...], preferred_element_type=F32)
        hid = gate * (1.0 / (1.0 + jnp.exp(-gate))) * up
        ys_ref[...] = jnp.dot(hid.astype(BF16), wd_bf[...], preferred_element_type=F32)

    @pl.when(b >= nv_ref[0])
    def _():
        ys_ref[...] = jnp.zeros_like(ys_ref)


def _expert_mlp(xs, block_expert, n_valid, wg, wu, wd, bm):
    n_blocks = xs.shape[0] // bm
    blk = lambda b, be, nv: (jnp.minimum(b, nv[0] - 1), 0)
    out_blk = lambda b, be, nv: (b, 0)
    wsel = lambda b, be, nv: (be[jnp.minimum(b, nv[0] - 1)], 0, 0)
    return pl.pallas_call(
        _expert_kernel,
        grid_spec=pltpu.PrefetchScalarGridSpec(
            num_scalar_prefetch=2,
            grid=(n_blocks,),
            in_specs=[pl.BlockSpec((bm, D_MODEL), blk),
                      pl.BlockSpec((None, D_MODEL, D_EXPERT), wsel),
                      pl.BlockSpec((None, D_MODEL, D_EXPERT), wsel),
                      pl.BlockSpec((None, D_EXPERT, D_MODEL), wsel)],
            out_specs=pl.BlockSpec((bm, D_MODEL), out_blk),
            scratch_shapes=[pltpu.VMEM((D_MODEL, D_EXPERT), BF16),
                            pltpu.VMEM((D_MODEL, D_EXPERT), BF16),
                            pltpu.VMEM((D_EXPERT, D_MODEL), BF16)],
        ),
        out_shape=jax.ShapeDtypeStruct(xs.shape, F32),
        compiler_params=pltpu.CompilerParams(
            dimension_semantics=("arbitrary",), vmem_limit_bytes=VMEM_LIMIT),
        name="moe_experts",
    )(block_expert, n_valid, xs, wg, wu, wd)


def _combine_kernel(x1_ref, slab_ref, dest_hbm, ys_hbm, x2_ref, idx_smem, gbuf, sem_idx, sem_rows):
    i = pl.program_id(0)
    ts = x1_ref.shape[0]
    idx_copy = pltpu.make_async_copy(dest_hbm.at[i], idx_smem, sem_idx)
    idx_copy.start()
    idx_copy.wait()

    def start(r, c):
        for k in range(2):
            _row_copy(ys_hbm, idx_smem[2 * r + k], gbuf.at[k], r, sem_rows).start()
        return c

    def wait(r, c):
        for k in range(2):
            _row_copy(ys_hbm, idx_smem[2 * r + k], gbuf.at[k], r, sem_rows).wait()
        return c

    lax.fori_loop(0, ts, start, 0)
    lax.fori_loop(0, ts, wait, 0)
    slab = slab_ref[...]
    y = slab[:, L_W1:L_W1 + 1] * gbuf[0] + slab[:, L_W2:L_W2 + 1] * gbuf[1]
    x2_ref[...] = x1_ref[...] + y


def _combine_rows(x1, slab, dest, ys):
    tokens = x1.shape[0]
    ts = dest.shape[1] // 2
    tok = lambda i: (i, 0)
    return pl.pallas_call(
        _combine_kernel,
        grid=(tokens // ts,),
        in_specs=[pl.BlockSpec((ts, D_MODEL), tok),
                  pl.BlockSpec((ts, LANES), tok),
                  pl.BlockSpec(memory_space=pl.ANY),
                  pl.BlockSpec(memory_space=pl.ANY)],
        out_specs=pl.BlockSpec((ts, D_MODEL), tok),
        out_shape=jax.ShapeDtypeStruct(x1.shape, F32),
        scratch_shapes=[pltpu.SMEM((2 * ts,), jnp.int32),
                        pltpu.VMEM((2, ts, D_MODEL), F32),
                        pltpu.SemaphoreType.DMA, pltpu.SemaphoreType.DMA],
        compiler_params=pltpu.CompilerParams(
            dimension_semantics=("arbitrary",), vmem_limit_bytes=VMEM_LIMIT),
        name="moe_combine",
    )(x1, slab, dest, ys)


def _moe(x1, h2, slab, counts, wg, wu, wd):
    tokens = x1.shape[0]
    bm = MOE_BLOCK
    ts = min(ROW_TILE, tokens)
    n_blocks = -(-2 * tokens // bm) + N_EXPERTS
    cnt = counts[0, :N_EXPERTS].astype(jnp.int32)
    padded = ((cnt + bm - 1) // bm) * bm
    pend = jnp.cumsum(padded)
    pstart = pend - padded
    eid = slab[:, L_E1:L_E2 + 1].astype(jnp.int32)
    rank = slab[:, L_R1:L_R2 + 1].astype(jnp.int32)
    dest = (pstart[eid] + rank).reshape(tokens // ts, 2 * ts)
    block_start = jnp.arange(n_blocks, dtype=jnp.int32) * bm
    block_expert = jnp.minimum(jnp.sum(pend[None, :] <= block_start[:, None], axis=1), N_EXPERTS - 1)
    n_valid = (pend[-1:] // bm).astype(jnp.int32)
    trailing = n_valid[0] + jnp.arange(N_EXPERTS, dtype=jnp.int32)
    zb = jnp.concatenate([pend - bm, trailing * bm]).astype(jnp.int32)
    zv = jnp.concatenate([padded > 0, trailing < n_blocks]).astype(jnp.int32)

    xs = _scatter_rows(h2, dest, zb, zv, n_blocks * bm, bm)
    ys = _expert_mlp(xs, block_expert.astype(jnp.int32), n_valid, wg, wu, wd, bm)
    return _combine_rows(x1, slab, dest, ys)


def _final_norm_kernel(x_ref, g_ref, o_ref):
    o_ref[...] = _rms(x_ref[...]) * g_ref[...]


def _final_norm(x, g, b0, nb):
    _, seq, _ = x.shape
    ts = min(ROW_TILE, seq)
    return pl.pallas_call(
        _final_norm_kernel,
        grid=(nb, seq // ts),
        in_specs=[pl.BlockSpec((None, ts, D_MODEL), lambda b, i: (b + b0, i, 0)),
                  _const_spec((1, D_MODEL))],
        out_specs=pl.BlockSpec((None, ts, D_MODEL), lambda b, i: (b, i, 0)),
        out_shape=jax.ShapeDtypeStruct((nb, seq, D_MODEL), F32),
        compiler_params=pltpu.CompilerParams(
            dimension_semantics=("arbitrary", "arbitrary"), vmem_limit_bytes=VMEM_LIMIT),
        name="final_norm",
    )(x, g)


def _rope_tables(seq):
    inv_freq = 1.0 / (ROPE_THETA ** (jnp.arange(0, QK_ROPE, 2, dtype=F32) / QK_ROPE))
    ang = jnp.arange(seq, dtype=F32)[:, None] * inv_freq[None, :]
    return jnp.cos(ang), jnp.sin(ang)


def _trunk(x, norm_mix, norm_ffn, pool_w, pool_scale,
           mla_w_dq, mla_q_norm, mla_w_uq, mla_w_dkv, mla_kv_norm, mla_w_ukv, mla_w_o,
           moe_w_group, moe_b_group, moe_w_router, moe_b_router, moe_w_gate, moe_w_up, moe_w_down):
    batch, seq, _ = x.shape
    ts = min(SEQ_TILE, seq)
    tri = jnp.tril(jnp.ones((ts, ts), F32), k=-1).astype(BF16)
    cos, sin = _rope_tables(seq)
    lane_pad = ((0, 0), (0, LANES - QK_ROPE))
    cos2 = jnp.pad(jnp.concatenate([cos, cos], axis=1), lane_pad)
    sin2 = jnp.pad(jnp.concatenate([-sin, sin], axis=1), lane_pad)
    cost, sint = cos.T, sin.T
    row = lambda v: v.reshape(1, -1)
    col = lambda v: v.reshape(-1, 1)

    for i in range(DEPTH):
        j = i // 2
        pad = LANES - N_EXPERTS - N_GROUPS
        wr = jnp.pad(jnp.concatenate([moe_w_router[i], moe_w_group[i]], axis=1), ((0, 0), (0, pad))).astype(BF16)
        br = row(jnp.pad(jnp.concatenate([moe_b_router[i], moe_b_group[i]]), (0, pad)))
        gmix, gffn = row(norm_mix[i]), row(norm_ffn[i])
        if i % 2 == 0:
            x1, h2, slab, counts = _pool_layer(x, gmix, pool_w[j].astype(BF16), row(pool_scale[j]),
                                               gffn, wr, br, tri)
        else:
            w_uq = mla_w_uq[j].reshape(Q_LORA, N_HEADS, QK_NOPE + QK_ROPE)
            w_ukv = mla_w_ukv[j].reshape(KV_LORA, N_HEADS, QK_NOPE + V_HEAD)
            wabst = _absorb_weights(w_ukv[:, :, :QK_NOPE].transpose(1, 0, 2),
                                    w_uq[:, :, :QK_NOPE].transpose(1, 2, 0))
            wqrt = w_uq[:, :, QK_NOPE:].transpose(1, 2, 0).astype(BF16)
            wvt = w_ukv[:, :, QK_NOPE:].transpose(1, 2, 0).astype(BF16)
            w_ckv = mla_w_dkv[j][:, :KV_LORA]
            w_kr = mla_w_dkv[j][:, KV_LORA:]
            half = QK_ROPE // 2
            w_kr_swapped = jnp.concatenate([w_kr[:, half:], w_kr[:, :half]], axis=1)
            wkv = jnp.concatenate([w_ckv, jnp.pad(w_kr, lane_pad), jnp.pad(w_kr_swapped, lane_pad)],
                                  axis=1).astype(BF16)
            cqt, ksh, ckvt = _mla_proj(x, gmix, mla_w_dq[j].T.astype(BF16), col(mla_q_norm[j]), wkv,
                                       w_ckv.T.astype(BF16), row(mla_kv_norm[j]), col(mla_kv_norm[j]),
                                       cos2, sin2)
            x1, h2, slab, counts = _attn_layer(x, cqt, ksh, ckvt, cost, sint, wabst, wqrt, wvt,
                                               mla_w_o[j].astype(BF16), gffn, wr, br, tri)
        x = _moe(x1, h2, slab, counts, moe_w_gate[i], moe_w_up[i], moe_w_down[i]).reshape(batch, seq, D_MODEL)
    return x


def kernel(x_prompt, x_sample, norm_mix, norm_ffn, norm_final, pool_w, pool_scale, mla_w_dq, mla_q_norm,
           mla_w_uq, mla_w_dkv, mla_kv_norm, mla_w_ukv, mla_w_o, moe_w_group, moe_b_group, moe_w_router,
           moe_b_router, moe_w_gate, moe_w_up, moe_w_down):
    n_prompt, n_sample = x_prompt.shape[0], x_sample.shape[0]
    x = jnp.concatenate([x_prompt, x_sample], axis=0)
    x = _trunk(x, norm_mix, norm_ffn, pool_w, pool_scale,
               mla_w_dq, mla_q_norm, mla_w_uq, mla_w_dkv, mla_kv_norm, mla_w_ukv, mla_w_o,
               moe_w_group, moe_b_group, moe_w_router, moe_b_router, moe_w_gate, moe_w_up, moe_w_down)
    g = norm_final.reshape(1, -1)
    return (_final_norm(x, g, 0, n_prompt), _final_norm(x, g, n_prompt, n_sample))
```

```python
import functools
import math

import jax
import jax.numpy as jnp
from jax import lax
from jax.experimental import pallas as pl
from jax.experimental.pallas import tpu as pltpu

F32 = jnp.float32
BF16 = jnp.bfloat16

D_MODEL = 1024
DEPTH = 4
POOL_WINDOWS = (2, 4, 8, 16)
POOL_GROUP = D_MODEL // len(POOL_WINDOWS)
POOL_HALO = 8
N_HEADS = 16
QK_NOPE = 64
QK_ROPE = 32
V_HEAD = 64
Q_LORA = 256
KV_LORA = 128
ROPE_THETA = 10000.0
N_GROUPS = 8
EXPERTS_PER_GROUP = 8
N_EXPERTS = N_GROUPS * EXPERTS_PER_GROUP
D_EXPERT = 512
EPS = 1e-6

LANES = 128
TOK_ROWS = D_MODEL // LANES
ROW_UNROLL = 8
IDX_SLOTS = 3
KEY_WIDTH = 2 * LANES
VT_ROWS = KV_LORA + 16
KEY_CHUNK = 256
SEQ_TILE = 256
ROW_TILE = 512
MOE_BLOCK = 256
VMEM_LIMIT = 48 * 1024 * 1024
SCORE_SCALE = (QK_NOPE + QK_ROPE) ** -0.5 * math.log2(math.e)

L_E1, L_E2, L_W1, L_W2, L_R1, L_R2 = range(6)


def _rms(v, axis=-1):
    return v * lax.rsqrt(jnp.mean(v * v, axis=axis, keepdims=True) + EPS)


def _to_token_tiles(ref, v):
    n = v.shape[0]
    for j in range(TOK_ROWS):
        ref[pl.ds(j, n, stride=TOK_ROWS), :] = v[:, j * LANES:(j + 1) * LANES]


def _from_token_tiles(ref, n):
    return jnp.concatenate([ref[pl.ds(j, n, stride=TOK_ROWS), :] for j in range(TOK_ROWS)], axis=1)


def _ffn_norm_and_route(x1, gffn_ref, wr_ref, br_ref, tri_ref, h2_ref, slab_ref, cnt_ref):
    h2 = _rms(x1) * gffn_ref[...]
    _to_token_tiles(h2_ref, h2)
    logits = jnp.dot(h2.astype(BF16), wr_ref[...], preferred_element_type=F32) + br_ref[...]
    lane = lax.broadcasted_iota(jnp.int32, logits.shape, 1)
    lanef = lane.astype(F32)
    neg = jnp.float32(-jnp.inf)
    big = jnp.float32(1e9)

    gmask = (lane >= N_EXPERTS) & (lane < N_EXPERTS + N_GROUPS)
    gl = jnp.where(gmask, logits, neg)
    gmax = jnp.max(gl, axis=-1, keepdims=True)
    gsum = jnp.sum(jnp.exp(gl - gmax), axis=-1, keepdims=True)
    g_p = 1.0 / gsum
    gidx = jnp.min(jnp.where(gl == gmax, lanef, big), axis=-1, keepdims=True) - N_EXPERTS
    lo = gidx * EXPERTS_PER_GROUP
    emask = (lanef >= lo) & (lanef < lo + EXPERTS_PER_GROUP)
    el = jnp.where(emask, logits, neg)
    m1 = jnp.max(el, axis=-1, keepdims=True)
    i1 = jnp.min(jnp.where(el == m1, lanef, big), axis=-1, keepdims=True)
    el2 = jnp.where(lanef == i1, neg, el)
    m2 = jnp.max(el2, axis=-1, keepdims=True)
    i2 = jnp.min(jnp.where(el2 == m2, lanef, big), axis=-1, keepdims=True)
    t = jnp.exp(m2 - m1)
    w1 = g_p / (1.0 + t)
    w2 = w1 * t

    oh1 = jnp.where(lanef == i1, 1.0, 0.0)
    oh2 = jnp.where(lanef == i2, 1.0, 0.0)
    oh = oh1 + oh2
    before = jnp.dot(tri_ref[...], oh.astype(BF16), preferred_element_type=F32) + cnt_ref[...]
    r1 = jnp.sum(oh1 * before, axis=-1, keepdims=True)
    r2 = jnp.sum(oh2 * before, axis=-1, keepdims=True)
    cnt_ref[...] = cnt_ref[...] + jnp.sum(oh, axis=0, keepdims=True)

    slab = jnp.zeros_like(logits)
    for l, v in ((L_E1, i1), (L_E2, i2), (L_W1, w1), (L_W2, w2), (L_R1, r1), (L_R2, r2)):
        slab = jnp.where(lane == l, v, slab)
    slab_ref[...] = slab


def _first_step():
    return (pl.program_id(0) == 0) & (pl.program_id(1) == 0)


def _pool_kernel(seq_len, batch_ends, *refs):
    n_src = len(batch_ends)
    src_refs, refs = refs[:3 * n_src], refs[3 * n_src:]
    (gmix_ref, pw_ref, ps_ref, gffn_ref, wr_ref, br_ref, tri_ref,
     x1_ref, h2_ref, slab_ref, cnt_ref, hcat_ref) = refs
    b = pl.program_id(0)
    i = pl.program_id(1)
    ts = x1_ref.shape[0]

    def pick(which):
        v = src_refs[3 * (n_src - 1) + which][...]
        for k in range(n_src - 2, -1, -1):
            v = jnp.where(b < batch_ends[k], src_refs[3 * k + which][...], v)
        return v

    @pl.when(_first_step())
    def _():
        cnt_ref[...] = jnp.zeros_like(cnt_ref)

    g = gmix_ref[...]
    x = pick(0)
    h = _rms(x) * g
    hp = jnp.where(i > 0, _rms(pick(1)) * g, 0.0)
    hn = jnp.where(i < pl.num_programs(1) - 1, _rms(pick(2)) * g, 0.0)
    hcat_ref[0:POOL_HALO, :] = hp
    hcat_ref[POOL_HALO:POOL_HALO + ts, :] = h
    hcat_ref[POOL_HALO + ts:2 * POOL_HALO + ts, :] = hn

    pos = i * ts + lax.broadcasted_iota(jnp.int32, (ts, 1), 0)
    for gi, w in enumerate(POOL_WINDOWS):
        c0 = gi * POOL_GROUP
        acc = None
        for j in range(-(w // 2), w // 2):
            v = hcat_ref[POOL_HALO + j:POOL_HALO + j + ts, c0:c0 + POOL_GROUP]
            acc = v if acc is None else acc + v
        cnt = (jnp.minimum(pos + w // 2, seq_len) - jnp.maximum(pos - w // 2, 0)).astype(F32)
        pooled = acc / cnt - h[:, c0:c0 + POOL_GROUP]
        mixed = jnp.dot(pooled.astype(BF16), pw_ref[gi], preferred_element_type=F32)
        x1_ref[:, c0:c0 + POOL_GROUP] = x[:, c0:c0 + POOL_GROUP] + mixed * ps_ref[:, c0:c0 + POOL_GROUP]

    _ffn_norm_and_route(x1_ref[...], gffn_ref, wr_ref, br_ref, tri_ref, h2_ref, slab_ref, cnt_ref)


def _tail_out(batch, seq, ts):
    tok = lambda b, i: (b * (seq // ts) + i, 0)
    shapes = (jax.ShapeDtypeStruct((batch * seq, D_MODEL), F32),
              jax.ShapeDtypeStruct((batch * seq * TOK_ROWS, LANES), F32),
              jax.ShapeDtypeStruct((batch * seq, LANES), F32),
              jax.ShapeDtypeStruct((1, LANES), F32))
    specs = (pl.BlockSpec((ts, D_MODEL), tok), pl.BlockSpec((ts * TOK_ROWS, LANES), tok),
             pl.BlockSpec((ts, LANES), tok), pl.BlockSpec((1, LANES), lambda b, i: (0, 0)))
    return shapes, specs


def _const_spec(shape):
    nd = len(shape)
    return pl.BlockSpec(shape, lambda b, i: (0,) * nd)


def _source_specs(first, count, ts, seq):
    hb = ts // POOL_HALO
    nh = seq // POOL_HALO

    def spec(rows, row_block):
        def index(b, i):
            mine = (b >= first) & (b < first + count)
            return (jnp.clip(b - first, 0, count - 1), jnp.where(mine, row_block(i), 0), 0)
        return pl.BlockSpec((None, rows, D_MODEL), index)

    return [spec(ts, lambda i: i),
            spec(POOL_HALO, lambda i: jnp.maximum(i * hb - 1, 0)),
            spec(POOL_HALO, lambda i: jnp.minimum((i + 1) * hb, nh - 1))]


def _pool_layer(sources, gmix, pw, ps, gffn, wr, br, tri):
    seq = sources[0].shape[1]
    ts = min(SEQ_TILE, seq)
    src_specs, src_args, ends, first = [], [], [], 0
    for src in sources:
        src_specs += _source_specs(first, src.shape[0], ts, seq)
        src_args += [src, src, src]
        first += src.shape[0]
        ends.append(first)
    batch = first
    out_shapes, out_specs = _tail_out(batch, seq, ts)
    return pl.pallas_call(
        functools.partial(_pool_kernel, seq, tuple(ends)),
        grid=(batch, seq // ts),
        in_specs=src_specs + [
            _const_spec((1, D_MODEL)),
            _const_spec((len(POOL_WINDOWS), POOL_GROUP, POOL_GROUP)),
            _const_spec((1, D_MODEL)),
            _const_spec((1, D_MODEL)),
            _const_spec((D_MODEL, LANES)),
            _const_spec((1, LANES)),
            _const_spec((ts, ts)),
        ],
        out_specs=out_specs,
        out_shape=out_shapes,
        scratch_shapes=[pltpu.VMEM((ts + 2 * POOL_HALO, D_MODEL), F32)],
        compiler_params=pltpu.CompilerParams(
            dimension_semantics=("arbitrary", "arbitrary"), vmem_limit_bytes=VMEM_LIMIT),
        name="pool_mixer",
    )(*src_args, gmix, pw, ps, gffn, wr, br, tri)


def _absorb_kernel(wk_ref, wqt_ref, out_ref):
    out_ref[...] = jnp.dot(wk_ref[...], wqt_ref[...], preferred_element_type=F32,
                           precision=lax.Precision.HIGHEST).astype(out_ref.dtype)


def _absorb_weights(wk, wqt):
    return pl.pallas_call(
        _absorb_kernel,
        grid=(N_HEADS,),
        in_specs=[pl.BlockSpec((None, KV_LORA, QK_NOPE), lambda h: (h, 0, 0)),
                  pl.BlockSpec((None, QK_NOPE, Q_LORA), lambda h: (h, 0, 0))],
        out_specs=pl.BlockSpec((None, KV_LORA, Q_LORA), lambda h: (h, 0, 0)),
        out_shape=jax.ShapeDtypeStruct((N_HEADS, KV_LORA, Q_LORA), BF16),
        name="absorb_weights",
    )(wk, wqt)


_NT = (((1,), (1,)), ((), ()))
_TN = (((0,), (0,)), ((), ()))


def _mla_proj_kernel(x_ref, gmix_ref, wdqt_ref, qn_ref, wkv_ref, wckvt_ref, kvn_row_ref, kvn_col_ref,
                     cos2_ref, sin2_ref, cqt_ref, ksh_ref, ckvt_ref):
    hb = (_rms(x_ref[...]) * gmix_ref[...]).astype(BF16)
    cqt = lax.dot_general(wdqt_ref[...], hb, _NT, preferred_element_type=F32)
    cqt_ref[...] = (_rms(cqt, axis=0) * qn_ref[...]).astype(BF16)
    kv = jnp.dot(hb, wkv_ref[...], preferred_element_type=F32)
    ckv = _rms(kv[:, 0:KV_LORA]) * kvn_row_ref[...]
    kr = kv[:, LANES:2 * LANES] * cos2_ref[...] + kv[:, 2 * LANES:3 * LANES] * sin2_ref[...]
    ksh_ref[:, 0:LANES] = ckv.astype(BF16)
    ksh_ref[:, LANES:2 * LANES] = kr.astype(BF16)
    ckvt = lax.dot_general(wckvt_ref[...], hb, _NT, preferred_element_type=F32)
    ckvt_ref[0:KV_LORA, :] = (_rms(ckvt, axis=0) * kvn_col_ref[...]).astype(BF16)
    ckvt_ref[KV_LORA:VT_ROWS, :] = jnp.ones((VT_ROWS - KV_LORA, ckvt.shape[1]), BF16)


def _mla_proj(x, gmix, wdqt, qn_col, wkv, wckvt, kvn_row, kvn_col, cos2, sin2):
    batch, seq, _ = x.shape
    ts = min(SEQ_TILE, seq)
    return pl.pallas_call(
        _mla_proj_kernel,
        grid=(batch, seq // ts),
        in_specs=[
            pl.BlockSpec((None, ts, D_MODEL), lambda b, i: (b, i, 0)),
            _const_spec((1, D_MODEL)),
            _const_spec((Q_LORA, D_MODEL)),
            _const_spec((Q_LORA, 1)),
            _const_spec((D_MODEL, 3 * LANES)),
            _const_spec((KV_LORA, D_MODEL)),
            _const_spec((1, KV_LORA)),
            _const_spec((KV_LORA, 1)),
            pl.BlockSpec((ts, LANES), lambda b, i: (i, 0)),
            pl.BlockSpec((ts, LANES), lambda b, i: (i, 0)),
        ],
        out_specs=(pl.BlockSpec((None, Q_LORA, ts), lambda b, i: (b, 0, i)),
                   pl.BlockSpec((None, ts, KEY_WIDTH), lambda b, i: (b, i, 0)),
                   pl.BlockSpec((None, VT_ROWS, ts), lambda b, i: (b, 0, i))),
        out_shape=(jax.ShapeDtypeStruct((batch, Q_LORA, seq), BF16),
                   jax.ShapeDtypeStruct((batch, seq, KEY_WIDTH), BF16),
                   jax.ShapeDtypeStruct((batch, VT_ROWS, seq), BF16)),
        compiler_params=pltpu.CompilerParams(
            dimension_semantics=("arbitrary", "arbitrary"), vmem_limit_bytes=VMEM_LIMIT),
        name="mla_proj",
    )(x, gmix, wdqt, qn_col, wkv, wckvt, kvn_row, kvn_col, cos2, sin2)


def _attn_kernel(x_ref, cqt_ref, ksh_ref, ckvt_ref, cost_ref, sint_ref, wabst_ref, wqrt_ref, wvt_ref, wo_ref,
                 gffn_ref, wr_ref, br_ref, tri_ref, x1_ref, h2_ref, slab_ref, cnt_ref, o_scr, s0_scr, s1_scr):
    tq = x_ref.shape[0]
    half = QK_ROPE // 2

    @pl.when(_first_step())
    def _():
        cnt_ref[...] = jnp.zeros_like(cnt_ref)

    cq = cqt_ref[...]
    cost = cost_ref[...]
    sint = sint_ref[...]
    pad = jnp.zeros((KEY_WIDTH - KV_LORA - QK_ROPE, tq), F32)

    n_chunks = ksh_ref.shape[0] // KEY_CHUNK

    def make_qp(h):
        qn = jnp.dot(wabst_ref[h], cq, preferred_element_type=F32)
        qr = jnp.dot(wqrt_ref[h], cq, preferred_element_type=F32)
        a, b = qr[0:half], qr[half:QK_ROPE]
        qp = jnp.concatenate([qn, a * cost - b * sint, b * cost + a * sint, pad], axis=0)
        return (qp * SCORE_SCALE).astype(BF16)

    def stage(h_out, mx_out, s_out, h_in, s_in):
        if h_in is not None:
            qp = make_qp(h_in)
            mx_in = jnp.full((8, tq), -jnp.inf, F32)
        else:
            mx_in = None
        if h_out is not None:
            m = jnp.max(mx_out, axis=0, keepdims=True)
            acc = jnp.zeros((VT_ROWS, tq), F32)
        for c in range(n_chunks):
            k0, k1 = c * KEY_CHUNK, (c + 1) * KEY_CHUNK
            if h_in is not None:
                s = jnp.dot(ksh_ref[k0:k1, :], qp, preferred_element_type=F32)
                s_in[k0:k1, :] = s
                mx_in = jnp.maximum(mx_in, jnp.max(s.reshape(KEY_CHUNK // 8, 8, tq), axis=0))
            if h_out is not None:
                p = jnp.exp2(s_out[k0:k1, :] - m).astype(BF16)
                acc = acc + jnp.dot(ckvt_ref[:, k0:k1], p, preferred_element_type=F32)
        if h_out is not None:
            on = acc[0:KV_LORA] * (1.0 / acc[KV_LORA:KV_LORA + 1])
            ov = jnp.dot(wvt_ref[h_out], on.astype(BF16), preferred_element_type=F32)
            o_scr[pl.ds(pl.multiple_of(h_out * V_HEAD, V_HEAD), V_HEAD), :] = ov.astype(BF16)
        return mx_in

    mx = stage(None, None, None, 0, s0_scr)

    def head_pair(j, mx):
        mx = stage(2 * j, mx, s0_scr, 2 * j + 1, s1_scr)
        return stage(2 * j + 1, mx, s1_scr, 2 * j + 2, s0_scr)

    mx = lax.fori_loop(0, N_HEADS // 2 - 1, head_pair, mx)
    mx = stage(N_HEADS - 2, mx, s0_scr, N_HEADS - 1, s1_scr)
    stage(N_HEADS - 1, mx, s1_scr, None, None)
    attn = lax.dot_general(o_scr[...], wo_ref[...], _TN, preferred_element_type=F32)
    x1 = x_ref[...] + attn
    x1_ref[...] = x1
    _ffn_norm_and_route(x1, gffn_ref, wr_ref, br_ref, tri_ref, h2_ref, slab_ref, cnt_ref)


def _attn_layer(x, cqt, ksh, ckvt, cost, sint, wabst, wqrt, wvt, wo, gffn, wr, br, tri):
    batch, seq, _ = x.shape
    tq = min(SEQ_TILE, seq)
    out_shapes, out_specs = _tail_out(batch, seq, tq)
    return pl.pallas_call(
        _attn_kernel,
        grid=(batch, seq // tq),
        in_specs=[
            pl.BlockSpec((None, tq, D_MODEL), lambda b, i: (b, i, 0)),
            pl.BlockSpec((None, Q_LORA, tq), lambda b, i: (b, 0, i)),
            pl.BlockSpec((None, seq, KEY_WIDTH), lambda b, i: (b, 0, 0)),
            pl.BlockSpec((None, VT_ROWS, seq), lambda b, i: (b, 0, 0)),
            pl.BlockSpec((QK_ROPE // 2, tq), lambda b, i: (0, i)),
            pl.BlockSpec((QK_ROPE // 2, tq), lambda b, i: (0, i)),
            _const_spec((N_HEADS, KV_LORA, Q_LORA)),
            _const_spec((N_HEADS, QK_ROPE, Q_LORA)),
            _const_spec((N_HEADS, V_HEAD, KV_LORA)),
            _const_spec((N_HEADS * V_HEAD, D_MODEL)),
            _const_spec((1, D_MODEL)),
            _const_spec((D_MODEL, LANES)),
            _const_spec((1, LANES)),
            _const_spec((tq, tq)),
        ],
        out_specs=out_specs,
        out_shape=out_shapes,
        scratch_shapes=[pltpu.VMEM((N_HEADS * V_HEAD, tq), BF16),
                        pltpu.VMEM((seq, tq), F32), pltpu.VMEM((seq, tq), F32)],
        compiler_params=pltpu.CompilerParams(
            dimension_semantics=("arbitrary", "arbitrary"), vmem_limit_bytes=VMEM_LIMIT),
        name="mla_attention",
    )(x, cqt, ksh, ckvt, cost, sint, wabst, wqrt, wvt, wo, gffn, wr, br, tri)


def _token_copy(src, src_tok, dst, dst_tok, sem):
    s = pl.multiple_of(src_tok * TOK_ROWS, TOK_ROWS)
    d = pl.multiple_of(dst_tok * TOK_ROWS, TOK_ROWS)
    return pltpu.make_async_copy(src.at[pl.ds(s, TOK_ROWS), :], dst.at[pl.ds(d, TOK_ROWS), :], sem)


def _for_each_token(ts, fn):
    def body(it, c):
        for u in range(ROW_UNROLL):
            fn(it * ROW_UNROLL + u)
        return c
    lax.fori_loop(0, ts // ROW_UNROLL, body, 0)


def _scatter_kernel(zb_ref, zv_ref, h2_hbm, dest_hbm, xs_hbm, idx_smem, zeros_vmem, sem_idx, sem_zero, sem_rows):
    i = pl.program_id(0)
    n_tiles = pl.num_programs(0)
    ts = idx_smem.shape[0] // (2 * IDX_SLOTS)
    zrows = zeros_vmem.shape[0]
    n_zero = zb_ref.shape[0]

    def zero_copy(e):
        start = pl.multiple_of(zb_ref[e] * TOK_ROWS, zrows)
        return pltpu.make_async_copy(zeros_vmem, xs_hbm.at[pl.ds(start, zrows), :], sem_zero)

    @pl.when(i == 0)
    def _():
        zeros_vmem[...] = jnp.zeros_like(zeros_vmem)

        def start(e, c):
            @pl.when(zv_ref[e] > 0)
            def _():
                zero_copy(e).start()
            return c

        def wait(e, c):
            @pl.when(zv_ref[e] > 0)
            def _():
                zero_copy(e).wait()
            return c

        lax.fori_loop(0, n_zero, start, 0)
        lax.fori_loop(0, n_zero, wait, 0)

    def idx_copy(t):
        slot = lax.rem(t, IDX_SLOTS)
        return pltpu.make_async_copy(dest_hbm.at[t], idx_smem.at[pl.ds(slot * (2 * ts), 2 * ts)], sem_idx.at[slot])

    def copies(t, r):
        slot = lax.rem(t, IDX_SLOTS)
        sem = sem_rows.at[lax.rem(t, 2)]
        return [_token_copy(h2_hbm, t * ts + r, xs_hbm, idx_smem[slot * (2 * ts) + 2 * r + k], sem) for k in range(2)]

    def start_tile(t):
        _for_each_token(ts, lambda r: [c.start(priority=k) for k, c in enumerate(copies(t, r))])

    def wait_tile(t):
        _for_each_token(ts, lambda r: [c.wait() for c in copies(t, r)])

    @pl.when(i == 0)
    def _():
        idx_copy(i).start()

    idx_copy(i).wait()

    @pl.when(i + 1 < n_tiles)
    def _():
        idx_copy(i + 1).start()

    start_tile(i)

    @pl.when(i > 0)
    def _():
        wait_tile(i - 1)

    @pl.when(i == n_tiles - 1)
    def _():
        wait_tile(i)


def _scatter_rows(h2, dest, zb, zv, n_rows, bm):
    ts = dest.shape[1] // 2
    tokens = h2.shape[0] // TOK_ROWS
    return pl.pallas_call(
        _scatter_kernel,
        grid_spec=pltpu.PrefetchScalarGridSpec(
            num_scalar_prefetch=2,
            grid=(tokens // ts,),
            in_specs=[pl.BlockSpec(memory_space=pl.ANY),
                      pl.BlockSpec(memory_space=pl.ANY)],
            out_specs=pl.BlockSpec(memory_space=pl.ANY),
            scratch_shapes=[pltpu.SMEM((IDX_SLOTS * 2 * ts,), jnp.int32),
                            pltpu.VMEM((bm * TOK_ROWS, LANES), F32),
                            pltpu.SemaphoreType.DMA((IDX_SLOTS,)),
                            pltpu.SemaphoreType.DMA, pltpu.SemaphoreType.DMA((2,))],
        ),
        out_shape=jax.ShapeDtypeStruct((n_rows * TOK_ROWS, LANES), F32),
        compiler_params=pltpu.CompilerParams(
            dimension_semantics=("arbitrary",), vmem_limit_bytes=VMEM_LIMIT),
        name="moe_scatter",
    )(zb, zv, h2, dest)


def _expert_kernel(be_ref, nv_ref, xs_ref, wg_ref, wu_ref, wd_ref, ys_ref, wg_bf, wu_bf, wd_bf):
    b = pl.program_id(0)

    @pl.when(b < nv_ref[0])
    def _():
        changed = (b == 0) | (be_ref[b] != be_ref[jnp.maximum(b - 1, 0)])

        @pl.when(changed)
        def _():
            wg_bf[...] = wg_ref[...].astype(BF16)
            wu_bf[...] = wu_ref[...].astype(BF16)
            wd_bf[...] = wd_ref[...].astype(BF16)

        xb = _from_token_tiles(xs_ref, xs_ref.shape[0] // TOK_ROWS).astype(BF16)
        gate = jnp.dot(xb, wg_bf[...], preferred_element_type=F32)
        up = jnp.dot(xb, wu_bf[...], preferred_element_type=F32)
        hid = gate * (1.0 / (1.0 + jnp.exp(-gate))) * up
        _to_token_tiles(ys_ref, jnp.dot(hid.astype(BF16), wd_bf[...], preferred_element_type=F32))

    @pl.when(b >= nv_ref[0])
    def _():
        ys_ref[...] = jnp.zeros_like(ys_ref)


def _expert_mlp(xs, block_expert, n_valid, wg, wu, wd, bm):
    n_blocks = xs.shape[0] // (bm * TOK_ROWS)
    blk = lambda b, be, nv: (jnp.minimum(b, nv[0] - 1), 0)
    out_blk = lambda b, be, nv: (b, 0)
    wsel = lambda b, be, nv: (be[jnp.minimum(b, nv[0] - 1)], 0, 0)
    return pl.pallas_call(
        _expert_kernel,
        grid_spec=pltpu.PrefetchScalarGridSpec(
            num_scalar_prefetch=2,
            grid=(n_blocks,),
            in_specs=[pl.BlockSpec((bm * TOK_ROWS, LANES), blk),
                      pl.BlockSpec((None, D_MODEL, D_EXPERT), wsel),
                      pl.BlockSpec((None, D_MODEL, D_EXPERT), wsel),
                      pl.BlockSpec((None, D_EXPERT, D_MODEL), wsel)],
            out_specs=pl.BlockSpec((bm * TOK_ROWS, LANES), out_blk),
            scratch_shapes=[pltpu.VMEM((D_MODEL, D_EXPERT), BF16),
                            pltpu.VMEM((D_MODEL, D_EXPERT), BF16),
                            pltpu.VMEM((D_EXPERT, D_MODEL), BF16)],
        ),
        out_shape=jax.ShapeDtypeStruct(xs.shape, F32),
        compiler_params=pltpu.CompilerParams(
            dimension_semantics=("arbitrary",), vmem_limit_bytes=VMEM_LIMIT),
        name="moe_experts",
    )(block_expert, n_valid, xs, wg, wu, wd)


def _combine_kernel(x1_ref, slab_ref, dest_hbm, ys_hbm, x2_ref, idx_smem, g0, g1, sem_idx, sem_rows):
    i = pl.program_id(0)
    n_tiles = pl.num_programs(0)
    ts = x1_ref.shape[0]

    def idx_copy(t):
        slot = lax.rem(t, IDX_SLOTS)
        return pltpu.make_async_copy(dest_hbm.at[t], idx_smem.at[pl.ds(slot * (2 * ts), 2 * ts)], sem_idx.at[slot])

    def copies(t, r):
        slot = lax.rem(t, IDX_SLOTS)
        half = lax.rem(t, 2)
        return [_token_copy(ys_hbm, idx_smem[slot * (2 * ts) + 2 * r + k], g.at[half], r, sem_rows.at[half])
                for k, g in enumerate((g0, g1))]

    def start_tile(t):
        _for_each_token(ts, lambda r: [c.start(priority=k) for k, c in enumerate(copies(t, r))])

    @pl.when(i == 0)
    def _():
        idx_copy(i).start()
        idx_copy(i).wait()
        start_tile(i)

        @pl.when(n_tiles > 1)
        def _():
            idx_copy(i + 1).start()

    @pl.when(i + 1 < n_tiles)
    def _():
        idx_copy(i + 1).wait()
        start_tile(i + 1)

    @pl.when(i + 2 < n_tiles)
    def _():
        idx_copy(i + 2).start()

    _for_each_token(ts, lambda r: [c.wait() for c in copies(i, r)])
    half = lax.rem(i, 2)
    slab = slab_ref[...]
    y = (slab[:, L_W1:L_W1 + 1] * _from_token_tiles(g0.at[half], ts)
         + slab[:, L_W2:L_W2 + 1] * _from_token_tiles(g1.at[half], ts))
    x2_ref[...] = x1_ref[...] + y


def _combine_rows(x1, slab, dest, ys):
    tokens = x1.shape[0]
    ts = dest.shape[1] // 2
    tok = lambda i: (i, 0)
    return pl.pallas_call(
        _combine_kernel,
        grid=(tokens // ts,),
        in_specs=[pl.BlockSpec((ts, D_MODEL), tok),
                  pl.BlockSpec((ts, LANES), tok),
                  pl.BlockSpec(memory_space=pl.ANY),
                  pl.BlockSpec(memory_space=pl.ANY)],
        out_specs=pl.BlockSpec((ts, D_MODEL), tok),
        out_shape=jax.ShapeDtypeStruct(x1.shape, F32),
        scratch_shapes=[pltpu.SMEM((IDX_SLOTS * 2 * ts,), jnp.int32),
                        pltpu.VMEM((2, ts * TOK_ROWS, LANES), F32), pltpu.VMEM((2, ts * TOK_ROWS, LANES), F32),
                        pltpu.SemaphoreType.DMA((IDX_SLOTS,)), pltpu.SemaphoreType.DMA((2,))],
        compiler_params=pltpu.CompilerParams(
            dimension_semantics=("arbitrary",), vmem_limit_bytes=VMEM_LIMIT),
        name="moe_combine",
    )(x1, slab, dest, ys)


def _moe(x1, h2, slab, counts, wg, wu, wd):
    tokens = x1.shape[0]
    bm = MOE_BLOCK
    ts = min(ROW_TILE, tokens)
    n_blocks = -(-2 * tokens // bm) + N_EXPERTS
    cnt = counts[0, :N_EXPERTS].astype(jnp.int32)
    padded = ((cnt + bm - 1) // bm) * bm
    pend = jnp.cumsum(padded)
    pstart = pend - padded
    eid = slab[:, L_E1:L_E2 + 1].astype(jnp.int32)
    rank = slab[:, L_R1:L_R2 + 1].astype(jnp.int32)
    hit = eid[:, :, None] == jnp.arange(N_EXPERTS, dtype=jnp.int32)
    dest = (jnp.sum(jnp.where(hit, pstart, 0), axis=-1) + rank).reshape(tokens // ts, 2 * ts)
    block_start = jnp.arange(n_blocks, dtype=jnp.int32) * bm
    block_expert = jnp.minimum(jnp.sum(pend[None, :] <= block_start[:, None], axis=1), N_EXPERTS - 1)
    n_valid = (pend[-1:] // bm).astype(jnp.int32)
    trailing = n_valid[0] + jnp.arange(N_EXPERTS, dtype=jnp.int32)
    zb = jnp.concatenate([pend - bm, trailing * bm]).astype(jnp.int32)
    zv = jnp.concatenate([padded > 0, trailing < n_blocks]).astype(jnp.int32)

    xs = _scatter_rows(h2, dest, zb, zv, n_blocks * bm, bm)
    ys = _expert_mlp(xs, block_expert.astype(jnp.int32), n_valid, wg, wu, wd, bm)
    return _combine_rows(x1, slab, dest, ys)


def _final_norm_kernel(x_ref, g_ref, o_ref):
    o_ref[...] = _rms(x_ref[...]) * g_ref[...]


def _final_norm(x, g, b0, nb):
    _, seq, _ = x.shape
    ts = min(ROW_TILE, seq)
    return pl.pallas_call(
        _final_norm_kernel,
        grid=(nb, seq // ts),
        in_specs=[pl.BlockSpec((None, ts, D_MODEL), lambda b, i: (b + b0, i, 0)),
                  _const_spec((1, D_MODEL))],
        out_specs=pl.BlockSpec((None, ts, D_MODEL), lambda b, i: (b, i, 0)),
        out_shape=jax.ShapeDtypeStruct((nb, seq, D_MODEL), F32),
        compiler_params=pltpu.CompilerParams(
            dimension_semantics=("arbitrary", "arbitrary"), vmem_limit_bytes=VMEM_LIMIT),
        name="final_norm",
    )(x, g)


def _rope_tables(seq):
    inv_freq = 1.0 / (ROPE_THETA ** (jnp.arange(0, QK_ROPE, 2, dtype=F32) / QK_ROPE))
    ang = jnp.arange(seq, dtype=F32)[:, None] * inv_freq[None, :]
    return jnp.cos(ang), jnp.sin(ang)


def _trunk(inputs, norm_mix, norm_ffn, pool_w, pool_scale,
           mla_w_dq, mla_q_norm, mla_w_uq, mla_w_dkv, mla_kv_norm, mla_w_ukv, mla_w_o,
           moe_w_group, moe_b_group, moe_w_router, moe_b_router, moe_w_gate, moe_w_up, moe_w_down):
    seq = inputs[0].shape[1]
    batch = sum(a.shape[0] for a in inputs)
    x = None
    ts = min(SEQ_TILE, seq)
    tri = jnp.tril(jnp.ones((ts, ts), F32), k=-1).astype(BF16)
    cos, sin = _rope_tables(seq)
    lane_pad = ((0, 0), (0, LANES - QK_ROPE))
    cos2 = jnp.pad(jnp.concatenate([cos, cos], axis=1), lane_pad)
    sin2 = jnp.pad(jnp.concatenate([-sin, sin], axis=1), lane_pad)
    cost, sint = cos.T, sin.T
    row = lambda v: v.reshape(1, -1)
    col = lambda v: v.reshape(-1, 1)

    for i in range(DEPTH):
        j = i // 2
        pad = LANES - N_EXPERTS - N_GROUPS
        wr = jnp.pad(jnp.concatenate([moe_w_router[i], moe_w_group[i]], axis=1), ((0, 0), (0, pad))).astype(BF16)
        br = row(jnp.pad(jnp.concatenate([moe_b_router[i], moe_b_group[i]]), (0, pad)))
        gmix, gffn = row(norm_mix[i]), row(norm_ffn[i])
        if i % 2 == 0:
            x1, h2, slab, counts = _pool_layer(inputs if x is None else [x], gmix, pool_w[j].astype(BF16),
                                               row(pool_scale[j]), gffn, wr, br, tri)
        else:
            w_uq = mla_w_uq[j].reshape(Q_LORA, N_HEADS, QK_NOPE + QK_ROPE)
            w_ukv = mla_w_ukv[j].reshape(KV_LORA, N_HEADS, QK_NOPE + V_HEAD)
            wabst = _absorb_weights(w_ukv[:, :, :QK_NOPE].transpose(1, 0, 2),
                                    w_uq[:, :, :QK_NOPE].transpose(1, 2, 0))
            wqrt = w_uq[:, :, QK_NOPE:].transpose(1, 2, 0).astype(BF16)
            wvt = w_ukv[:, :, QK_NOPE:].transpose(1, 2, 0).astype(BF16)
            w_ckv = mla_w_dkv[j][:, :KV_LORA]
            w_kr = mla_w_dkv[j][:, KV_LORA:]
            half = QK_ROPE // 2
            w_kr_swapped = jnp.concatenate([w_kr[:, half:], w_kr[:, :half]], axis=1)
            wkv = jnp.concatenate([w_ckv, jnp.pad(w_kr, lane_pad), jnp.pad(w_kr_swapped, lane_pad)],
                                  axis=1).astype(BF16)
            cqt, ksh, ckvt = _mla_proj(x, gmix, mla_w_dq[j].T.astype(BF16), col(mla_q_norm[j]), wkv,
                                       w_ckv.T.astype(BF16), row(mla_kv_norm[j]), col(mla_kv_norm[j]),
                                       cos2, sin2)
            x1, h2, slab, counts = _attn_layer(x, cqt, ksh, ckvt, cost, sint, wabst, wqrt, wvt,
                                               mla_w_o[j].astype(BF16), gffn, wr, br, tri)
        x = _moe(x1, h2, slab, counts, moe_w_gate[i], moe_w_up[i], moe_w_down[i]).reshape(batch, seq, D_MODEL)
    return x


def kernel(x_prompt, x_sample, norm_mix, norm_ffn, norm_final, pool_w, pool_scale, mla_w_dq, mla_q_norm,
           mla_w_uq, mla_w_dkv, mla_kv_norm, mla_w_ukv, mla_w_o, moe_w_group, moe_b_group, moe_w_router,
           moe_b_router, moe_w_gate, moe_w_up, moe_w_down):
    n_prompt, n_sample = x_prompt.shape[0], x_sample.shape[0]
    x = _trunk([x_prompt, x_sample], norm_mix, norm_ffn, pool_w, pool_scale,
               mla_w_dq, mla_q_norm, mla_w_uq, mla_w_dkv, mla_kv_norm, mla_w_ukv, mla_w_o,
               moe_w_group, moe_b_group, moe_w_router, moe_b_router, moe_w_gate, moe_w_up, moe_w_down)
    g = norm_final.reshape(1, -1)
    return (_final_norm(x, g, 0, n_prompt), _final_norm(x, g, n_prompt, n_sample))
```

```python
import functools
import math

import jax
import jax.numpy as jnp
from jax import lax
from jax.experimental import pallas as pl
from jax.experimental.pallas import tpu as pltpu

F32 = jnp.float32
BF16 = jnp.bfloat16

D_MODEL = 1024
DEPTH = 4
POOL_WINDOWS = (2, 4, 8, 16)
POOL_GROUP = D_MODEL // len(POOL_WINDOWS)
POOL_HALO = 8
N_HEADS = 16
QK_NOPE = 64
QK_ROPE = 32
V_HEAD = 64
Q_LORA = 256
KV_LORA = 128
ROPE_THETA = 10000.0
N_GROUPS = 8
EXPERTS_PER_GROUP = 8
N_EXPERTS = N_GROUPS * EXPERTS_PER_GROUP
D_EXPERT = 512
EPS = 1e-6

LANES = 128
TOK_ROWS = D_MODEL // LANES
ROW_UNROLL = 8
IDX_SLOTS = 3
KEY_WIDTH = 2 * LANES
VT_ROWS = KV_LORA + 16
KEY_CHUNK = 256
PV_CHUNKS = 2
SEQ_TILE = 256
ROW_TILE = 512
MOE_BLOCK = 256
VMEM_LIMIT = 48 * 1024 * 1024
SCORE_SCALE = (QK_NOPE + QK_ROPE) ** -0.5 * math.log2(math.e)

L_E1, L_E2, L_W1, L_W2, L_R1, L_R2 = range(6)


def _rms(v, axis=-1):
    return v * lax.rsqrt(jnp.mean(v * v, axis=axis, keepdims=True) + EPS)


def _to_token_tiles(ref, v):
    n = v.shape[0]
    for j in range(TOK_ROWS):
        ref[pl.ds(j, n, stride=TOK_ROWS), :] = v[:, j * LANES:(j + 1) * LANES]


def _from_token_tiles(ref, n):
    return jnp.concatenate([ref[pl.ds(j, n, stride=TOK_ROWS), :] for j in range(TOK_ROWS)], axis=1)


def _ffn_norm_and_route(x1, gffn_ref, wr_ref, br_ref, tri_ref, h2_ref, slab_ref, cnt_ref):
    h2 = _rms(x1) * gffn_ref[...]
    _to_token_tiles(h2_ref, h2)
    logits = jnp.dot(h2.astype(BF16), wr_ref[...], preferred_element_type=F32) + br_ref[...]
    lane = lax.broadcasted_iota(jnp.int32, logits.shape, 1)
    lanef = lane.astype(F32)
    neg = jnp.float32(-jnp.inf)
    big = jnp.float32(1e9)

    gmask = (lane >= N_EXPERTS) & (lane < N_EXPERTS + N_GROUPS)
    gl = jnp.where(gmask, logits, neg)
    gmax = jnp.max(gl, axis=-1, keepdims=True)
    gsum = jnp.sum(jnp.exp(gl - gmax), axis=-1, keepdims=True)
    g_p = 1.0 / gsum
    gidx = jnp.min(jnp.where(gl == gmax, lanef, big), axis=-1, keepdims=True) - N_EXPERTS
    lo = gidx * EXPERTS_PER_GROUP
    emask = (lanef >= lo) & (lanef < lo + EXPERTS_PER_GROUP)
    el = jnp.where(emask, logits, neg)
    m1 = jnp.max(el, axis=-1, keepdims=True)
    i1 = jnp.min(jnp.where(el == m1, lanef, big), axis=-1, keepdims=True)
    el2 = jnp.where(lanef == i1, neg, el)
    m2 = jnp.max(el2, axis=-1, keepdims=True)
    i2 = jnp.min(jnp.where(el2 == m2, lanef, big), axis=-1, keepdims=True)
    t = jnp.exp(m2 - m1)
    w1 = g_p / (1.0 + t)
    w2 = w1 * t

    oh1 = jnp.where(lanef == i1, 1.0, 0.0)
    oh2 = jnp.where(lanef == i2, 1.0, 0.0)
    oh = oh1 + oh2
    before = jnp.dot(tri_ref[...], oh.astype(BF16), preferred_element_type=F32) + cnt_ref[...]
    r1 = jnp.sum(oh1 * before, axis=-1, keepdims=True)
    r2 = jnp.sum(oh2 * before, axis=-1, keepdims=True)
    cnt_ref[...] = cnt_ref[...] + jnp.sum(oh, axis=0, keepdims=True)

    slab = jnp.zeros_like(logits)
    for l, v in ((L_E1, i1), (L_E2, i2), (L_W1, w1), (L_W2, w2), (L_R1, r1), (L_R2, r2)):
        slab = jnp.where(lane == l, v, slab)
    slab_ref[...] = slab


def _first_step():
    return (pl.program_id(0) == 0) & (pl.program_id(1) == 0)


def _pool_kernel(seq_len, batch_ends, *refs):
    n_src = len(batch_ends)
    src_refs, refs = refs[:3 * n_src], refs[3 * n_src:]
    (gmix_ref, pw_ref, ps_ref, gffn_ref, wr_ref, br_ref, tri_ref,
     x1_ref, h2_ref, slab_ref, cnt_ref, hcat_ref) = refs
    b = pl.program_id(0)
    i = pl.program_id(1)
    ts = x1_ref.shape[0]

    def pick(which):
        v = src_refs[3 * (n_src - 1) + which][...]
        for k in range(n_src - 2, -1, -1):
            v = jnp.where(b < batch_ends[k], src_refs[3 * k + which][...], v)
        return v

    @pl.when(_first_step())
    def _():
        cnt_ref[...] = jnp.zeros_like(cnt_ref)

    g = gmix_ref[...]
    x = pick(0)
    h = _rms(x) * g
    hp = jnp.where(i > 0, _rms(pick(1)) * g, 0.0)
    hn = jnp.where(i < pl.num_programs(1) - 1, _rms(pick(2)) * g, 0.0)
    hcat_ref[0:POOL_HALO, :] = hp
    hcat_ref[POOL_HALO:POOL_HALO + ts, :] = h
    hcat_ref[POOL_HALO + ts:2 * POOL_HALO + ts, :] = hn

    pos = i * ts + lax.broadcasted_iota(jnp.int32, (ts, 1), 0)
    for gi, w in enumerate(POOL_WINDOWS):
        c0 = gi * POOL_GROUP
        acc = None
        for j in range(-(w // 2), w // 2):
            v = hcat_ref[POOL_HALO + j:POOL_HALO + j + ts, c0:c0 + POOL_GROUP]
            acc = v if acc is None else acc + v
        cnt = (jnp.minimum(pos + w // 2, seq_len) - jnp.maximum(pos - w // 2, 0)).astype(F32)
        pooled = acc / cnt - h[:, c0:c0 + POOL_GROUP]
        mixed = jnp.dot(pooled.astype(BF16), pw_ref[gi], preferred_element_type=F32)
        x1_ref[:, c0:c0 + POOL_GROUP] = x[:, c0:c0 + POOL_GROUP] + mixed * ps_ref[:, c0:c0 + POOL_GROUP]

    _ffn_norm_and_route(x1_ref[...], gffn_ref, wr_ref, br_ref, tri_ref, h2_ref, slab_ref, cnt_ref)


def _tail_out(batch, seq, ts):
    tok = lambda b, i: (b * (seq // ts) + i, 0)
    shapes = (jax.ShapeDtypeStruct((batch * seq, D_MODEL), F32),
              jax.ShapeDtypeStruct((batch * seq * TOK_ROWS, LANES), F32),
              jax.ShapeDtypeStruct((batch * seq, LANES), F32),
              jax.ShapeDtypeStruct((1, LANES), F32))
    specs = (pl.BlockSpec((ts, D_MODEL), tok), pl.BlockSpec((ts * TOK_ROWS, LANES), tok),
             pl.BlockSpec((ts, LANES), tok), pl.BlockSpec((1, LANES), lambda b, i: (0, 0)))
    return shapes, specs


def _const_spec(shape):
    nd = len(shape)
    return pl.BlockSpec(shape, lambda b, i: (0,) * nd)


def _source_specs(first, count, ts, seq):
    hb = ts // POOL_HALO
    nh = seq // POOL_HALO

    def spec(rows, row_block):
        def index(b, i):
            mine = (b >= first) & (b < first + count)
            return (jnp.clip(b - first, 0, count - 1), jnp.where(mine, row_block(i), 0), 0)
        return pl.BlockSpec((None, rows, D_MODEL), index)

    return [spec(ts, lambda i: i),
            spec(POOL_HALO, lambda i: jnp.maximum(i * hb - 1, 0)),
            spec(POOL_HALO, lambda i: jnp.minimum((i + 1) * hb, nh - 1))]


def _pool_layer(sources, gmix, pw, ps, gffn, wr, br, tri):
    seq = sources[0].shape[1]
    ts = min(SEQ_TILE, seq)
    src_specs, src_args, ends, first = [], [], [], 0
    for src in sources:
        src_specs += _source_specs(first, src.shape[0], ts, seq)
        src_args += [src, src, src]
        first += src.shape[0]
        ends.append(first)
    batch = first
    out_shapes, out_specs = _tail_out(batch, seq, ts)
    return pl.pallas_call(
        functools.partial(_pool_kernel, seq, tuple(ends)),
        grid=(batch, seq // ts),
        in_specs=src_specs + [
            _const_spec((1, D_MODEL)),
            _const_spec((len(POOL_WINDOWS), POOL_GROUP, POOL_GROUP)),
            _const_spec((1, D_MODEL)),
            _const_spec((1, D_MODEL)),
            _const_spec((D_MODEL, LANES)),
            _const_spec((1, LANES)),
            _const_spec((ts, ts)),
        ],
        out_specs=out_specs,
        out_shape=out_shapes,
        scratch_shapes=[pltpu.VMEM((ts + 2 * POOL_HALO, D_MODEL), F32)],
        compiler_params=pltpu.CompilerParams(
            dimension_semantics=("arbitrary", "arbitrary"), vmem_limit_bytes=VMEM_LIMIT),
        name="pool_mixer",
    )(*src_args, gmix, pw, ps, gffn, wr, br, tri)


def _absorb_kernel(wk_ref, wqt_ref, out_ref):
    out_ref[...] = jnp.dot(wk_ref[...], wqt_ref[...], preferred_element_type=F32,
                           precision=lax.Precision.HIGHEST).astype(out_ref.dtype)


def _absorb_weights(wk, wqt):
    return pl.pallas_call(
        _absorb_kernel,
        grid=(N_HEADS,),
        in_specs=[pl.BlockSpec((None, KV_LORA, QK_NOPE), lambda h: (h, 0, 0)),
                  pl.BlockSpec((None, QK_NOPE, Q_LORA), lambda h: (h, 0, 0))],
        out_specs=pl.BlockSpec((None, KV_LORA, Q_LORA), lambda h: (h, 0, 0)),
        out_shape=jax.ShapeDtypeStruct((N_HEADS, KV_LORA, Q_LORA), BF16),
        name="absorb_weights",
    )(wk, wqt)


_NT = (((1,), (1,)), ((), ()))
_TN = (((0,), (0,)), ((), ()))


def _mla_proj_kernel(x_ref, gmix_ref, wdqt_ref, qn_ref, wkv_ref, wckvt_ref, kvn_row_ref, kvn_col_ref,
                     cos2_ref, sin2_ref, cqt_ref, ksh_ref, ckvt_ref):
    hb = (_rms(x_ref[...]) * gmix_ref[...]).astype(BF16)
    cqt = lax.dot_general(wdqt_ref[...], hb, _NT, preferred_element_type=F32)
    cqt_ref[...] = (_rms(cqt, axis=0) * qn_ref[...]).astype(BF16)
    kv = jnp.dot(hb, wkv_ref[...], preferred_element_type=F32)
    ckv = _rms(kv[:, 0:KV_LORA]) * kvn_row_ref[...]
    kr = kv[:, LANES:2 * LANES] * cos2_ref[...] + kv[:, 2 * LANES:3 * LANES] * sin2_ref[...]
    ksh_ref[:, 0:LANES] = ckv.astype(BF16)
    ksh_ref[:, LANES:2 * LANES] = kr.astype(BF16)
    ckvt = lax.dot_general(wckvt_ref[...], hb, _NT, preferred_element_type=F32)
    ckvt_ref[0:KV_LORA, :] = (_rms(ckvt, axis=0) * kvn_col_ref[...]).astype(BF16)
    ckvt_ref[KV_LORA:VT_ROWS, :] = jnp.ones((VT_ROWS - KV_LORA, ckvt.shape[1]), BF16)


def _mla_proj(x, gmix, wdqt, qn_col, wkv, wckvt, kvn_row, kvn_col, cos2, sin2):
    batch, seq, _ = x.shape
    ts = min(SEQ_TILE, seq)
    return pl.pallas_call(
        _mla_proj_kernel,
        grid=(batch, seq // ts),
        in_specs=[
            pl.BlockSpec((None, ts, D_MODEL), lambda b, i: (b, i, 0)),
            _const_spec((1, D_MODEL)),
            _const_spec((Q_LORA, D_MODEL)),
            _const_spec((Q_LORA, 1)),
            _const_spec((D_MODEL, 3 * LANES)),
            _const_spec((KV_LORA, D_MODEL)),
            _const_spec((1, KV_LORA)),
            _const_spec((KV_LORA, 1)),
            pl.BlockSpec((ts, LANES), lambda b, i: (i, 0)),
            pl.BlockSpec((ts, LANES), lambda b, i: (i, 0)),
        ],
        out_specs=(pl.BlockSpec((None, Q_LORA, ts), lambda b, i: (b, 0, i)),
                   pl.BlockSpec((None, ts, KEY_WIDTH), lambda b, i: (b, i, 0)),
                   pl.BlockSpec((None, VT_ROWS, ts), lambda b, i: (b, 0, i))),
        out_shape=(jax.ShapeDtypeStruct((batch, Q_LORA, seq), BF16),
                   jax.ShapeDtypeStruct((batch, seq, KEY_WIDTH), BF16),
                   jax.ShapeDtypeStruct((batch, VT_ROWS, seq), BF16)),
        compiler_params=pltpu.CompilerParams(
            dimension_semantics=("arbitrary", "arbitrary"), vmem_limit_bytes=VMEM_LIMIT),
        name="mla_proj",
    )(x, gmix, wdqt, qn_col, wkv, wckvt, kvn_row, kvn_col, cos2, sin2)


def _attn_kernel(x_ref, cqt_ref, ksh_ref, ckvt_ref, cost_ref, sint_ref, wabst_ref, wqrt_ref, wvt_ref, wo_ref,
                 gffn_ref, wr_ref, br_ref, tri_ref, x1_ref, h2_ref, slab_ref, cnt_ref, o_scr, s0_scr, s1_scr):
    tq = x_ref.shape[0]
    half = QK_ROPE // 2

    @pl.when(_first_step())
    def _():
        cnt_ref[...] = jnp.zeros_like(cnt_ref)

    cq = cqt_ref[...]
    cost = cost_ref[...]
    sint = sint_ref[...]
    pad = jnp.zeros((KEY_WIDTH - KV_LORA - QK_ROPE, tq), F32)

    n_chunks = ksh_ref.shape[0] // KEY_CHUNK

    def make_qp(h):
        qn = jnp.dot(wabst_ref[h], cq, preferred_element_type=F32)
        qr = jnp.dot(wqrt_ref[h], cq, preferred_element_type=F32)
        a, b = qr[0:half], qr[half:QK_ROPE]
        qp = jnp.concatenate([qn, a * cost - b * sint, b * cost + a * sint, pad], axis=0)
        return (qp * SCORE_SCALE).astype(BF16)

    def stage(h_out, mx_out, s_out, h_in, s_in):
        if h_in is not None:
            qp = make_qp(h_in)
            mx_in = jnp.full((8, tq), -jnp.inf, F32)
        else:
            mx_in = None
        if h_out is not None:
            m = jnp.max(mx_out, axis=0, keepdims=True)
            acc = jnp.zeros((VT_ROWS, tq), F32)
        for c in range(n_chunks):
            k0, k1 = c * KEY_CHUNK, (c + 1) * KEY_CHUNK
            if h_in is not None:
                s = jnp.dot(ksh_ref[k0:k1, :], qp, preferred_element_type=F32)
                s_in[k0:k1, :] = s
                mx_in = jnp.maximum(mx_in, jnp.max(s.reshape(KEY_CHUNK // 8, 8, tq), axis=0))
            if h_out is not None and (c + 1) % PV_CHUNKS == 0:
                k0 = k1 - PV_CHUNKS * KEY_CHUNK
                p = jnp.exp2(s_out[k0:k1, :] - m).astype(BF16)
                acc = acc + jnp.dot(ckvt_ref[:, k0:k1], p, preferred_element_type=F32)
        if h_out is not None:
            on = acc[0:KV_LORA] * (1.0 / acc[KV_LORA:KV_LORA + 1])
            ov = jnp.dot(wvt_ref[h_out], on.astype(BF16), preferred_element_type=F32)
            o_scr[pl.ds(pl.multiple_of(h_out * V_HEAD, V_HEAD), V_HEAD), :] = ov.astype(BF16)
        return mx_in

    mx = stage(None, None, None, 0, s0_scr)

    def head_pair(j, mx):
        mx = stage(2 * j, mx, s0_scr, 2 * j + 1, s1_scr)
        return stage(2 * j + 1, mx, s1_scr, 2 * j + 2, s0_scr)

    mx = lax.fori_loop(0, N_HEADS // 2 - 1, head_pair, mx)
    mx = stage(N_HEADS - 2, mx, s0_scr, N_HEADS - 1, s1_scr)
    stage(N_HEADS - 1, mx, s1_scr, None, None)
    attn = lax.dot_general(o_scr[...], wo_ref[...], _TN, preferred_element_type=F32)
    x1 = x_ref[...] + attn
    x1_ref[...] = x1
    _ffn_norm_and_route(x1, gffn_ref, wr_ref, br_ref, tri_ref, h2_ref, slab_ref, cnt_ref)


def _attn_layer(x, cqt, ksh, ckvt, cost, sint, wabst, wqrt, wvt, wo, gffn, wr, br, tri):
    batch, seq, _ = x.shape
    tq = min(SEQ_TILE, seq)
    out_shapes, out_specs = _tail_out(batch, seq, tq)
    return pl.pallas_call(
        _attn_kernel,
        grid=(batch, seq // tq),
        in_specs=[
            pl.BlockSpec((None, tq, D_MODEL), lambda b, i: (b, i, 0)),
            pl.BlockSpec((None, Q_LORA, tq), lambda b, i: (b, 0, i)),
            pl.BlockSpec((None, seq, KEY_WIDTH), lambda b, i: (b, 0, 0)),
            pl.BlockSpec((None, VT_ROWS, seq), lambda b, i: (b, 0, 0)),
            pl.BlockSpec((QK_ROPE // 2, tq), lambda b, i: (0, i)),
            pl.BlockSpec((QK_ROPE // 2, tq), lambda b, i: (0, i)),
            _const_spec((N_HEADS, KV_LORA, Q_LORA)),
            _const_spec((N_HEADS, QK_ROPE, Q_LORA)),
            _const_spec((N_HEADS, V_HEAD, KV_LORA)),
            _const_spec((N_HEADS * V_HEAD, D_MODEL)),
            _const_spec((1, D_MODEL)),
            _const_spec((D_MODEL, LANES)),
            _const_spec((1, LANES)),
            _const_spec((tq, tq)),
        ],
        out_specs=out_specs,
        out_shape=out_shapes,
        scratch_shapes=[pltpu.VMEM((N_HEADS * V_HEAD, tq), BF16),
                        pltpu.VMEM((seq, tq), F32), pltpu.VMEM((seq, tq), F32)],
        compiler_params=pltpu.CompilerParams(
            dimension_semantics=("arbitrary", "arbitrary"), vmem_limit_bytes=VMEM_LIMIT),
        name="mla_attention",
    )(x, cqt, ksh, ckvt, cost, sint, wabst, wqrt, wvt, wo, gffn, wr, br, tri)


def _token_copy(src, src_tok, dst, dst_tok, sem):
    s = pl.multiple_of(src_tok * TOK_ROWS, TOK_ROWS)
    d = pl.multiple_of(dst_tok * TOK_ROWS, TOK_ROWS)
    return pltpu.make_async_copy(src.at[pl.ds(s, TOK_ROWS), :], dst.at[pl.ds(d, TOK_ROWS), :], sem)


def _for_each_token(ts, fn):
    def body(it, c):
        for u in range(ROW_UNROLL):
            fn(it * ROW_UNROLL + u)
        return c
    lax.fori_loop(0, ts // ROW_UNROLL, body, 0)


def _scatter_kernel(zb_ref, zv_ref, h2_hbm, dest_hbm, xs_hbm, idx_smem, stage, zeros_vmem,
                    sem_idx, sem_stage, sem_zero, sem_rows):
    i = pl.program_id(0)
    n_tiles = pl.num_programs(0)
    ts = idx_smem.shape[0] // (2 * IDX_SLOTS)
    tile_rows = ts * TOK_ROWS
    zrows = zeros_vmem.shape[0]
    n_zero = zb_ref.shape[0]

    def zero_copy(e):
        start = pl.multiple_of(zb_ref[e] * TOK_ROWS, zrows)
        return pltpu.make_async_copy(zeros_vmem, xs_hbm.at[pl.ds(start, zrows), :], sem_zero)

    @pl.when(i == 0)
    def _():
        zeros_vmem[...] = jnp.zeros_like(zeros_vmem)

        def start(e, c):
            @pl.when(zv_ref[e] > 0)
            def _():
                zero_copy(e).start()
            return c

        def wait(e, c):
            @pl.when(zv_ref[e] > 0)
            def _():
                zero_copy(e).wait()
            return c

        lax.fori_loop(0, n_zero, start, 0)
        lax.fori_loop(0, n_zero, wait, 0)

    def fetches(t):
        slot = lax.rem(t, IDX_SLOTS)
        rows = h2_hbm.at[pl.ds(pl.multiple_of(t * tile_rows, tile_rows), tile_rows), :]
        return [pltpu.make_async_copy(dest_hbm.at[t], idx_smem.at[pl.ds(slot * (2 * ts), 2 * ts)], sem_idx.at[slot]),
                pltpu.make_async_copy(rows, stage.at[slot], sem_stage.at[slot])]

    def copies(t, r):
        slot = lax.rem(t, IDX_SLOTS)
        sem = sem_rows.at[lax.rem(t, 2)]
        return [_token_copy(stage.at[slot], r, xs_hbm, idx_smem[slot * (2 * ts) + 2 * r + k], sem)
                for k in range(2)]

    def start_tile(t):
        _for_each_token(ts, lambda r: [c.start(priority=k) for k, c in enumerate(copies(t, r))])

    def wait_tile(t):
        _for_each_token(ts, lambda r: [c.wait() for c in copies(t, r)])

    @pl.when(i == 0)
    def _():
        for c in fetches(i):
            c.start()

    for c in fetches(i):
        c.wait()

    @pl.when(i + 1 < n_tiles)
    def _():
        for c in fetches(i + 1):
            c.start()

    start_tile(i)

    @pl.when(i > 0)
    def _():
        wait_tile(i - 1)

    @pl.when(i == n_tiles - 1)
    def _():
        wait_tile(i)


def _scatter_rows(h2, dest, zb, zv, n_rows, bm):
    ts = dest.shape[1] // 2
    tokens = h2.shape[0] // TOK_ROWS
    return pl.pallas_call(
        _scatter_kernel,
        grid_spec=pltpu.PrefetchScalarGridSpec(
            num_scalar_prefetch=2,
            grid=(tokens // ts,),
            in_specs=[pl.BlockSpec(memory_space=pl.ANY),
                      pl.BlockSpec(memory_space=pl.ANY)],
            out_specs=pl.BlockSpec(memory_space=pl.ANY),
            scratch_shapes=[pltpu.SMEM((IDX_SLOTS * 2 * ts,), jnp.int32),
                            pltpu.VMEM((IDX_SLOTS, ts * TOK_ROWS, LANES), F32),
                            pltpu.VMEM((bm * TOK_ROWS, LANES), F32),
                            pltpu.SemaphoreType.DMA((IDX_SLOTS,)), pltpu.SemaphoreType.DMA((IDX_SLOTS,)),
                            pltpu.SemaphoreType.DMA, pltpu.SemaphoreType.DMA((2,))],
        ),
        out_shape=jax.ShapeDtypeStruct((n_rows * TOK_ROWS, LANES), F32),
        compiler_params=pltpu.CompilerParams(
            dimension_semantics=("arbitrary",), vmem_limit_bytes=VMEM_LIMIT),
        name="moe_scatter",
    )(zb, zv, h2, dest)


def _expert_kernel(be_ref, nv_ref, xs_ref, wg_ref, wu_ref, wd_ref, ys_ref, wg_bf, wu_bf, wd_bf):
    b = pl.program_id(0)

    @pl.when(b < nv_ref[0])
    def _():
        changed = (b == 0) | (be_ref[b] != be_ref[jnp.maximum(b - 1, 0)])

        @pl.when(changed)
        def _():
            wg_bf[...] = wg_ref[...].astype(BF16)
            wu_bf[...] = wu_ref[...].astype(BF16)
            wd_bf[...] = wd_ref[...].astype(BF16)

        xb = _from_token_tiles(xs_ref, xs_ref.shape[0] // TOK_ROWS).astype(BF16)
        gate = jnp.dot(xb, wg_bf[...], preferred_element_type=F32)
        up = jnp.dot(xb, wu_bf[...], preferred_element_type=F32)
        hid = gate * (1.0 / (1.0 + jnp.exp(-gate))) * up
        _to_token_tiles(ys_ref, jnp.dot(hid.astype(BF16), wd_bf[...], preferred_element_type=F32))

    @pl.when(b >= nv_ref[0])
    def _():
        ys_ref[...] = jnp.zeros_like(ys_ref)


def _expert_mlp(xs, block_expert, n_valid, wg, wu, wd, layer, bm):
    n_blocks = xs.shape[0] // (bm * TOK_ROWS)
    blk = lambda b, be, nv: (jnp.minimum(b, nv[0] - 1), 0)
    out_blk = lambda b, be, nv: (b, 0)
    wsel = lambda b, be, nv: (layer, be[jnp.minimum(b, nv[0] - 1)], 0, 0)
    return pl.pallas_call(
        _expert_kernel,
        grid_spec=pltpu.PrefetchScalarGridSpec(
            num_scalar_prefetch=2,
            grid=(n_blocks,),
            in_specs=[pl.BlockSpec((bm * TOK_ROWS, LANES), blk),
                      pl.BlockSpec((None, None, D_MODEL, D_EXPERT), wsel),
                      pl.BlockSpec((None, None, D_MODEL, D_EXPERT), wsel),
                      pl.BlockSpec((None, None, D_EXPERT, D_MODEL), wsel)],
            out_specs=pl.BlockSpec((bm * TOK_ROWS, LANES), out_blk),
            scratch_shapes=[pltpu.VMEM((D_MODEL, D_EXPERT), BF16),
                            pltpu.VMEM((D_MODEL, D_EXPERT), BF16),
                            pltpu.VMEM((D_EXPERT, D_MODEL), BF16)],
        ),
        out_shape=jax.ShapeDtypeStruct(xs.shape, F32),
        compiler_params=pltpu.CompilerParams(
            dimension_semantics=("arbitrary",), vmem_limit_bytes=VMEM_LIMIT),
        name="moe_experts",
    )(block_expert, n_valid, xs, wg, wu, wd)


def _combine_kernel(x1_ref, slab_ref, dest_hbm, ys_hbm, x2_ref, idx_smem, g0, g1, sem_idx, sem_rows):
    i = pl.program_id(0)
    n_tiles = pl.num_programs(0)
    ts = x1_ref.shape[0]

    def idx_copy(t):
        slot = lax.rem(t, IDX_SLOTS)
        return pltpu.make_async_copy(dest_hbm.at[t], idx_smem.at[pl.ds(slot * (2 * ts), 2 * ts)], sem_idx.at[slot])

    def copies(t, r):
        slot = lax.rem(t, IDX_SLOTS)
        half = lax.rem(t, 2)
        return [_token_copy(ys_hbm, idx_smem[slot * (2 * ts) + 2 * r + k], g.at[half], r, sem_rows.at[half])
                for k, g in enumerate((g0, g1))]

    def start_tile(t):
        _for_each_token(ts, lambda r: [c.start(priority=k) for k, c in enumerate(copies(t, r))])

    @pl.when(i == 0)
    def _():
        idx_copy(i).start()
        idx_copy(i).wait()
        start_tile(i)

        @pl.when(n_tiles > 1)
        def _():
            idx_copy(i + 1).start()

    @pl.when(i + 1 < n_tiles)
    def _():
        idx_copy(i + 1).wait()
        start_tile(i + 1)

    @pl.when(i + 2 < n_tiles)
    def _():
        idx_copy(i + 2).start()

    _for_each_token(ts, lambda r: [c.wait() for c in copies(i, r)])
    half = lax.rem(i, 2)
    slab = slab_ref[...]
    y = (slab[:, L_W1:L_W1 + 1] * _from_token_tiles(g0.at[half], ts)
         + slab[:, L_W2:L_W2 + 1] * _from_token_tiles(g1.at[half], ts))
    x2_ref[...] = x1_ref[...] + y


def _combine_rows(x1, slab, dest, ys):
    tokens = x1.shape[0]
    ts = dest.shape[1] // 2
    tok = lambda i: (i, 0)
    return pl.pallas_call(
        _combine_kernel,
        grid=(tokens // ts,),
        in_specs=[pl.BlockSpec((ts, D_MODEL), tok),
                  pl.BlockSpec((ts, LANES), tok),
                  pl.BlockSpec(memory_space=pl.ANY),
                  pl.BlockSpec(memory_space=pl.ANY)],
        out_specs=pl.BlockSpec((ts, D_MODEL), tok),
        out_shape=jax.ShapeDtypeStruct(x1.shape, F32),
        scratch_shapes=[pltpu.SMEM((IDX_SLOTS * 2 * ts,), jnp.int32),
                        pltpu.VMEM((2, ts * TOK_ROWS, LANES), F32), pltpu.VMEM((2, ts * TOK_ROWS, LANES), F32),
                        pltpu.SemaphoreType.DMA((IDX_SLOTS,)), pltpu.SemaphoreType.DMA((2,))],
        compiler_params=pltpu.CompilerParams(
            dimension_semantics=("arbitrary",), vmem_limit_bytes=VMEM_LIMIT),
        name="moe_combine",
    )(x1, slab, dest, ys)


def _moe(x1, h2, slab, counts, wg, wu, wd, layer):
    tokens = x1.shape[0]
    bm = MOE_BLOCK
    ts = min(ROW_TILE, tokens)
    n_blocks = -(-2 * tokens // bm) + N_EXPERTS
    cnt = counts[0, :N_EXPERTS].astype(jnp.int32)
    padded = ((cnt + bm - 1) // bm) * bm
    pend = jnp.cumsum(padded)
    pstart = pend - padded
    eid = slab[:, L_E1:L_E2 + 1].astype(jnp.int32)
    rank = slab[:, L_R1:L_R2 + 1].astype(jnp.int32)
    hit = eid[:, :, None] == jnp.arange(N_EXPERTS, dtype=jnp.int32)
    dest = (jnp.sum(jnp.where(hit, pstart, 0), axis=-1) + rank).reshape(tokens // ts, 2 * ts)
    block_start = jnp.arange(n_blocks, dtype=jnp.int32) * bm
    block_expert = jnp.minimum(jnp.sum(pend[None, :] <= block_start[:, None], axis=1), N_EXPERTS - 1)
    n_valid = (pend[-1:] // bm).astype(jnp.int32)
    trailing = n_valid[0] + jnp.arange(N_EXPERTS, dtype=jnp.int32)
    zb = jnp.concatenate([pend - bm, trailing * bm]).astype(jnp.int32)
    zv = jnp.concatenate([padded > 0, trailing < n_blocks]).astype(jnp.int32)

    xs = _scatter_rows(h2, dest, zb, zv, n_blocks * bm, bm)
    ys = _expert_mlp(xs, block_expert.astype(jnp.int32), n_valid, wg, wu, wd, layer, bm)
    return _combine_rows(x1, slab, dest, ys)


def _final_norm_kernel(x_ref, g_ref, o_ref):
    o_ref[...] = _rms(x_ref[...]) * g_ref[...]


def _final_norm(x, g, b0, nb):
    _, seq, _ = x.shape
    ts = min(ROW_TILE, seq)
    return pl.pallas_call(
        _final_norm_kernel,
        grid=(nb, seq // ts),
        in_specs=[pl.BlockSpec((None, ts, D_MODEL), lambda b, i: (b + b0, i, 0)),
                  _const_spec((1, D_MODEL))],
        out_specs=pl.BlockSpec((None, ts, D_MODEL), lambda b, i: (b, i, 0)),
        out_shape=jax.ShapeDtypeStruct((nb, seq, D_MODEL), F32),
        compiler_params=pltpu.CompilerParams(
            dimension_semantics=("arbitrary", "arbitrary"), vmem_limit_bytes=VMEM_LIMIT),
        name="final_norm",
    )(x, g)


def _rope_tables(seq):
    inv_freq = 1.0 / (ROPE_THETA ** (jnp.arange(0, QK_ROPE, 2, dtype=F32) / QK_ROPE))
    ang = jnp.arange(seq, dtype=F32)[:, None] * inv_freq[None, :]
    return jnp.cos(ang), jnp.sin(ang)


def _trunk(inputs, norm_mix, norm_ffn, pool_w, pool_scale,
           mla_w_dq, mla_q_norm, mla_w_uq, mla_w_dkv, mla_kv_norm, mla_w_ukv, mla_w_o,
           moe_w_group, moe_b_group, moe_w_router, moe_b_router, moe_w_gate, moe_w_up, moe_w_down):
    seq = inputs[0].shape[1]
    batch = sum(a.shape[0] for a in inputs)
    x = None
    ts = min(SEQ_TILE, seq)
    tri = jnp.tril(jnp.ones((ts, ts), F32), k=-1).astype(BF16)
    cos, sin = _rope_tables(seq)
    lane_pad = ((0, 0), (0, LANES - QK_ROPE))
    cos2 = jnp.pad(jnp.concatenate([cos, cos], axis=1), lane_pad)
    sin2 = jnp.pad(jnp.concatenate([-sin, sin], axis=1), lane_pad)
    cost, sint = cos.T, sin.T
    row = lambda v: v.reshape(1, -1)
    col = lambda v: v.reshape(-1, 1)

    for i in range(DEPTH):
        j = i // 2
        pad = LANES - N_EXPERTS - N_GROUPS
        wr = jnp.pad(jnp.concatenate([moe_w_router[i], moe_w_group[i]], axis=1), ((0, 0), (0, pad))).astype(BF16)
        br = row(jnp.pad(jnp.concatenate([moe_b_router[i], moe_b_group[i]]), (0, pad)))
        gmix, gffn = row(norm_mix[i]), row(norm_ffn[i])
        if i % 2 == 0:
            x1, h2, slab, counts = _pool_layer(inputs if x is None else [x], gmix, pool_w[j].astype(BF16),
                                               row(pool_scale[j]), gffn, wr, br, tri)
        else:
            w_uq = mla_w_uq[j].reshape(Q_LORA, N_HEADS, QK_NOPE + QK_ROPE)
            w_ukv = mla_w_ukv[j].reshape(KV_LORA, N_HEADS, QK_NOPE + V_HEAD)
            wabst = _absorb_weights(w_ukv[:, :, :QK_NOPE].transpose(1, 0, 2),
                                    w_uq[:, :, :QK_NOPE].transpose(1, 2, 0))
            wqrt = w_uq[:, :, QK_NOPE:].transpose(1, 2, 0).astype(BF16)
            wvt = w_ukv[:, :, QK_NOPE:].transpose(1, 2, 0).astype(BF16)
            w_ckv = mla_w_dkv[j][:, :KV_LORA]
            w_kr = mla_w_dkv[j][:, KV_LORA:]
            half = QK_ROPE // 2
            w_kr_swapped = jnp.concatenate([w_kr[:, half:], w_kr[:, :half]], axis=1)
            wkv = jnp.concatenate([w_ckv, jnp.pad(w_kr, lane_pad), jnp.pad(w_kr_swapped, lane_pad)],
                                  axis=1).astype(BF16)
            cqt, ksh, ckvt = _mla_proj(x, gmix, mla_w_dq[j].T.astype(BF16), col(mla_q_norm[j]), wkv,
                                       w_ckv.T.astype(BF16), row(mla_kv_norm[j]), col(mla_kv_norm[j]),
                                       cos2, sin2)
            x1, h2, slab, counts = _attn_layer(x, cqt, ksh, ckvt, cost, sint, wabst, wqrt, wvt,
                                               mla_w_o[j].astype(BF16), gffn, wr, br, tri)
        x = _moe(x1, h2, slab, counts, moe_w_gate, moe_w_up, moe_w_down, i).reshape(batch, seq, D_MODEL)
    return x


def kernel(x_prompt, x_sample, norm_mix, norm_ffn, norm_final, pool_w, pool_scale, mla_w_dq, mla_q_norm,
           mla_w_uq, mla_w_dkv, mla_kv_norm, mla_w_ukv, mla_w_o, moe_w_group, moe_b_group, moe_w_router,
           moe_b_router, moe_w_gate, moe_w_up, moe_w_down):
    n_prompt, n_sample = x_prompt.shape[0], x_sample.shape[0]
    x = _trunk([x_prompt, x_sample], norm_mix, norm_ffn, pool_w, pool_scale,
               mla_w_dq, mla_q_norm, mla_w_uq, mla_w_dkv, mla_kv_norm, mla_w_ukv, mla_w_o,
               moe_w_group, moe_b_group, moe_w_router, moe_b_router, moe_w_gate, moe_w_up, moe_w_down)
    g = norm_final.reshape(1, -1)
    return (_final_norm(x, g, 0, n_prompt), _final_norm(x, g, n_prompt, n_sample))
```

```python
import functools
import math

import jax
import jax.numpy as jnp
from jax import lax
from jax.experimental import pallas as pl
from jax.experimental.pallas import tpu as pltpu

F32 = jnp.float32
BF16 = jnp.bfloat16

D_MODEL = 1024
DEPTH = 4
POOL_WINDOWS = (2, 4, 8, 16)
POOL_GROUP = D_MODEL // len(POOL_WINDOWS)
POOL_HALO = 8
N_HEADS = 16
QK_NOPE = 64
QK_ROPE = 32
V_HEAD = 64
Q_LORA = 256
KV_LORA = 128
ROPE_THETA = 10000.0
N_GROUPS = 8
EXPERTS_PER_GROUP = 8
N_EXPERTS = N_GROUPS * EXPERTS_PER_GROUP
D_EXPERT = 512
EPS = 1e-6

LANES = 128
TOK_ROWS = D_MODEL // LANES
ROW_UNROLL = 8
IDX_SLOTS = 3
KEY_WIDTH = 2 * LANES
VT_ROWS = KV_LORA + 16
KEY_CHUNK = 256
PV_CHUNKS = 2
SEQ_TILE = 256
ROW_TILE = 512
MOE_BLOCK = 512
VMEM_LIMIT = 48 * 1024 * 1024
SCORE_SCALE = (QK_NOPE + QK_ROPE) ** -0.5 * math.log2(math.e)

L_E1, L_E2, L_W1, L_W2, L_R1, L_R2 = range(6)


def _rms(v, axis=-1):
    return v * lax.rsqrt(jnp.mean(v * v, axis=axis, keepdims=True) + EPS)


def _to_token_tiles(ref, v):
    n = v.shape[0]
    for j in range(TOK_ROWS):
        ref[pl.ds(j, n, stride=TOK_ROWS), :] = v[:, j * LANES:(j + 1) * LANES]


def _from_token_tiles(ref, n):
    return jnp.concatenate([ref[pl.ds(j, n, stride=TOK_ROWS), :] for j in range(TOK_ROWS)], axis=1)


def _ffn_norm_and_route(x1, gffn_ref, wr_ref, br_ref, tri_ref, h2_ref, slab_ref, cnt_ref):
    h2 = _rms(x1) * gffn_ref[...]
    _to_token_tiles(h2_ref, h2)
    logits = jnp.dot(h2.astype(BF16), wr_ref[...], preferred_element_type=F32) + br_ref[...]
    lane = lax.broadcasted_iota(jnp.int32, logits.shape, 1)
    lanef = lane.astype(F32)
    neg = jnp.float32(-jnp.inf)
    big = jnp.float32(1e9)

    gmask = (lane >= N_EXPERTS) & (lane < N_EXPERTS + N_GROUPS)
    gl = jnp.where(gmask, logits, neg)
    gmax = jnp.max(gl, axis=-1, keepdims=True)
    gsum = jnp.sum(jnp.exp(gl - gmax), axis=-1, keepdims=True)
    g_p = 1.0 / gsum
    gidx = jnp.min(jnp.where(gl == gmax, lanef, big), axis=-1, keepdims=True) - N_EXPERTS
    lo = gidx * EXPERTS_PER_GROUP
    emask = (lanef >= lo) & (lanef < lo + EXPERTS_PER_GROUP)
    el = jnp.where(emask, logits, neg)
    m1 = jnp.max(el, axis=-1, keepdims=True)
    i1 = jnp.min(jnp.where(el == m1, lanef, big), axis=-1, keepdims=True)
    el2 = jnp.where(lanef == i1, neg, el)
    m2 = jnp.max(el2, axis=-1, keepdims=True)
    i2 = jnp.min(jnp.where(el2 == m2, lanef, big), axis=-1, keepdims=True)
    t = jnp.exp(m2 - m1)
    w1 = g_p / (1.0 + t)
    w2 = w1 * t

    oh1 = jnp.where(lanef == i1, 1.0, 0.0)
    oh2 = jnp.where(lanef == i2, 1.0, 0.0)
    oh = oh1 + oh2
    before = jnp.dot(tri_ref[...], oh.astype(BF16), preferred_element_type=F32) + cnt_ref[...]
    r1 = jnp.sum(oh1 * before, axis=-1, keepdims=True)
    r2 = jnp.sum(oh2 * before, axis=-1, keepdims=True)
    cnt_ref[...] = cnt_ref[...] + jnp.sum(oh, axis=0, keepdims=True)

    slab = jnp.zeros_like(logits)
    for l, v in ((L_E1, i1), (L_E2, i2), (L_W1, w1), (L_W2, w2), (L_R1, r1), (L_R2, r2)):
        slab = jnp.where(lane == l, v, slab)
    slab_ref[...] = slab


def _first_step():
    return (pl.program_id(0) == 0) & (pl.program_id(1) == 0)


def _pool_kernel(seq_len, batch_ends, *refs):
    n_src = len(batch_ends)
    src_refs, refs = refs[:3 * n_src], refs[3 * n_src:]
    (gmix_ref, pw_ref, ps_ref, gffn_ref, wr_ref, br_ref, tri_ref,
     x1_ref, h2_ref, slab_ref, cnt_ref, hcat_ref) = refs
    b = pl.program_id(0)
    i = pl.program_id(1)
    ts = x1_ref.shape[0]

    def pick(which):
        v = src_refs[3 * (n_src - 1) + which][...]
        for k in range(n_src - 2, -1, -1):
            v = jnp.where(b < batch_ends[k], src_refs[3 * k + which][...], v)
        return v

    @pl.when(_first_step())
    def _():
        cnt_ref[...] = jnp.zeros_like(cnt_ref)

    g = gmix_ref[...]
    x = pick(0)
    h = _rms(x) * g
    hp = jnp.where(i > 0, _rms(pick(1)) * g, 0.0)
    hn = jnp.where(i < pl.num_programs(1) - 1, _rms(pick(2)) * g, 0.0)
    hcat_ref[0:POOL_HALO, :] = hp
    hcat_ref[POOL_HALO:POOL_HALO + ts, :] = h
    hcat_ref[POOL_HALO + ts:2 * POOL_HALO + ts, :] = hn

    pos = i * ts + lax.broadcasted_iota(jnp.int32, (ts, 1), 0)
    for gi, w in enumerate(POOL_WINDOWS):
        c0 = gi * POOL_GROUP
        acc = None
        for j in range(-(w // 2), w // 2):
            v = hcat_ref[POOL_HALO + j:POOL_HALO + j + ts, c0:c0 + POOL_GROUP]
            acc = v if acc is None else acc + v
        cnt = (jnp.minimum(pos + w // 2, seq_len) - jnp.maximum(pos - w // 2, 0)).astype(F32)
        pooled = acc / cnt - h[:, c0:c0 + POOL_GROUP]
        mixed = jnp.dot(pooled.astype(BF16), pw_ref[gi], preferred_element_type=F32)
        x1_ref[:, c0:c0 + POOL_GROUP] = x[:, c0:c0 + POOL_GROUP] + mixed * ps_ref[:, c0:c0 + POOL_GROUP]

    _ffn_norm_and_route(x1_ref[...], gffn_ref, wr_ref, br_ref, tri_ref, h2_ref, slab_ref, cnt_ref)


def _tail_out(batch, seq, ts):
    tok = lambda b, i: (b * (seq // ts) + i, 0)
    shapes = (jax.ShapeDtypeStruct((batch * seq, D_MODEL), F32),
              jax.ShapeDtypeStruct((batch * seq * TOK_ROWS, LANES), F32),
              jax.ShapeDtypeStruct((batch * seq, LANES), F32),
              jax.ShapeDtypeStruct((1, LANES), F32))
    specs = (pl.BlockSpec((ts, D_MODEL), tok), pl.BlockSpec((ts * TOK_ROWS, LANES), tok),
             pl.BlockSpec((ts, LANES), tok), pl.BlockSpec((1, LANES), lambda b, i: (0, 0)))
    return shapes, specs


def _const_spec(shape):
    nd = len(shape)
    return pl.BlockSpec(shape, lambda b, i: (0,) * nd)


def _source_specs(first, count, ts, seq):
    hb = ts // POOL_HALO
    nh = seq // POOL_HALO

    def spec(rows, row_block):
        def index(b, i):
            mine = (b >= first) & (b < first + count)
            return (jnp.clip(b - first, 0, count - 1), jnp.where(mine, row_block(i), 0), 0)
        return pl.BlockSpec((None, rows, D_MODEL), index)

    return [spec(ts, lambda i: i),
            spec(POOL_HALO, lambda i: jnp.maximum(i * hb - 1, 0)),
            spec(POOL_HALO, lambda i: jnp.minimum((i + 1) * hb, nh - 1))]


def _pool_layer(sources, gmix, pw, ps, gffn, wr, br, tri):
    seq = sources[0].shape[1]
    ts = min(SEQ_TILE, seq)
    src_specs, src_args, ends, first = [], [], [], 0
    for src in sources:
        src_specs += _source_specs(first, src.shape[0], ts, seq)
        src_args += [src, src, src]
        first += src.shape[0]
        ends.append(first)
    batch = first
    out_shapes, out_specs = _tail_out(batch, seq, ts)
    return pl.pallas_call(
        functools.partial(_pool_kernel, seq, tuple(ends)),
        grid=(batch, seq // ts),
        in_specs=src_specs + [
            _const_spec((1, D_MODEL)),
            _const_spec((len(POOL_WINDOWS), POOL_GROUP, POOL_GROUP)),
            _const_spec((1, D_MODEL)),
            _const_spec((1, D_MODEL)),
            _const_spec((D_MODEL, LANES)),
            _const_spec((1, LANES)),
            _const_spec((ts, ts)),
        ],
        out_specs=out_specs,
        out_shape=out_shapes,
        scratch_shapes=[pltpu.VMEM((ts + 2 * POOL_HALO, D_MODEL), F32)],
        compiler_params=pltpu.CompilerParams(
            dimension_semantics=("arbitrary", "arbitrary"), vmem_limit_bytes=VMEM_LIMIT),
        name="pool_mixer",
    )(*src_args, gmix, pw, ps, gffn, wr, br, tri)


def _absorb_kernel(wk_ref, wqt_ref, out_ref):
    out_ref[...] = jnp.dot(wk_ref[...], wqt_ref[...], preferred_element_type=F32,
                           precision=lax.Precision.HIGHEST).astype(out_ref.dtype)


def _absorb_weights(wk, wqt):
    return pl.pallas_call(
        _absorb_kernel,
        grid=(N_HEADS,),
        in_specs=[pl.BlockSpec((None, KV_LORA, QK_NOPE), lambda h: (h, 0, 0)),
                  pl.BlockSpec((None, QK_NOPE, Q_LORA), lambda h: (h, 0, 0))],
        out_specs=pl.BlockSpec((None, KV_LORA, Q_LORA), lambda h: (h, 0, 0)),
        out_shape=jax.ShapeDtypeStruct((N_HEADS, KV_LORA, Q_LORA), BF16),
        name="absorb_weights",
    )(wk, wqt)


_NT = (((1,), (1,)), ((), ()))
_TN = (((0,), (0,)), ((), ()))


def _mla_proj_kernel(x_ref, gmix_ref, wdqt_ref, qn_ref, wkv_ref, wckvt_ref, kvn_row_ref, kvn_col_ref,
                     cos2_ref, sin2_ref, cqt_ref, ksh_ref, ckvt_ref):
    hb = (_rms(x_ref[...]) * gmix_ref[...]).astype(BF16)
    cqt = lax.dot_general(wdqt_ref[...], hb, _NT, preferred_element_type=F32)
    cqt_ref[...] = (_rms(cqt, axis=0) * qn_ref[...]).astype(BF16)
    kv = jnp.dot(hb, wkv_ref[...], preferred_element_type=F32)
    ckv = _rms(kv[:, 0:KV_LORA]) * kvn_row_ref[...]
    kr = kv[:, LANES:2 * LANES] * cos2_ref[...] + kv[:, 2 * LANES:3 * LANES] * sin2_ref[...]
    ksh_ref[:, 0:LANES] = ckv.astype(BF16)
    ksh_ref[:, LANES:2 * LANES] = kr.astype(BF16)
    ckvt = lax.dot_general(wckvt_ref[...], hb, _NT, preferred_element_type=F32)
    ckvt_ref[0:KV_LORA, :] = (_rms(ckvt, axis=0) * kvn_col_ref[...]).astype(BF16)
    ckvt_ref[KV_LORA:VT_ROWS, :] = jnp.ones((VT_ROWS - KV_LORA, ckvt.shape[1]), BF16)


def _mla_proj(x, gmix, wdqt, qn_col, wkv, wckvt, kvn_row, kvn_col, cos2, sin2):
    batch, seq, _ = x.shape
    ts = min(SEQ_TILE, seq)
    return pl.pallas_call(
        _mla_proj_kernel,
        grid=(batch, seq // ts),
        in_specs=[
            pl.BlockSpec((None, ts, D_MODEL), lambda b, i: (b, i, 0)),
            _const_spec((1, D_MODEL)),
            _const_spec((Q_LORA, D_MODEL)),
            _const_spec((Q_LORA, 1)),
            _const_spec((D_MODEL, 3 * LANES)),
            _const_spec((KV_LORA, D_MODEL)),
            _const_spec((1, KV_LORA)),
            _const_spec((KV_LORA, 1)),
            pl.BlockSpec((ts, LANES), lambda b, i: (i, 0)),
            pl.BlockSpec((ts, LANES), lambda b, i: (i, 0)),
        ],
        out_specs=(pl.BlockSpec((None, Q_LORA, ts), lambda b, i: (b, 0, i)),
                   pl.BlockSpec((None, ts, KEY_WIDTH), lambda b, i: (b, i, 0)),
                   pl.BlockSpec((None, VT_ROWS, ts), lambda b, i: (b, 0, i))),
        out_shape=(jax.ShapeDtypeStruct((batch, Q_LORA, seq), BF16),
                   jax.ShapeDtypeStruct((batch, seq, KEY_WIDTH), BF16),
                   jax.ShapeDtypeStruct((batch, VT_ROWS, seq), BF16)),
        compiler_params=pltpu.CompilerParams(
            dimension_semantics=("arbitrary", "arbitrary"), vmem_limit_bytes=VMEM_LIMIT),
        name="mla_proj",
    )(x, gmix, wdqt, qn_col, wkv, wckvt, kvn_row, kvn_col, cos2, sin2)


def _attn_kernel(x_ref, cqt_ref, ksh_ref, ckvt_ref, cost_ref, sint_ref, wabst_ref, wqrt_ref, wvt_ref, wo_ref,
                 gffn_ref, wr_ref, br_ref, tri_ref, x1_ref, h2_ref, slab_ref, cnt_ref,
                 o_scr, s0_scr, s1_scr, q0_scr, q1_scr):
    tq = x_ref.shape[0]
    half = QK_ROPE // 2

    @pl.when(_first_step())
    def _():
        cnt_ref[...] = jnp.zeros_like(cnt_ref)

    cq = cqt_ref[...]
    cost = cost_ref[...]
    sint = sint_ref[...]
    pad = jnp.zeros((KEY_WIDTH - KV_LORA - QK_ROPE, tq), F32)

    n_chunks = ksh_ref.shape[0] // KEY_CHUNK

    def make_qp(h):
        qn = jnp.dot(wabst_ref[h], cq, preferred_element_type=F32)
        qr = jnp.dot(wqrt_ref[h], cq, preferred_element_type=F32)
        a, b = qr[0:half], qr[half:QK_ROPE]
        qp = jnp.concatenate([qn, a * cost - b * sint, b * cost + a * sint, pad], axis=0)
        return (qp * SCORE_SCALE).astype(BF16)

    def stage(h_out, mx_out, s_out, h_in, s_in, q_in, h_next, q_next):
        if h_next is not None:
            q_next[...] = make_qp(h_next)
        if h_in is not None:
            qp = q_in[...]
            mx_in = jnp.full((8, tq), -jnp.inf, F32)
        else:
            mx_in = None
        if h_out is not None:
            m = jnp.max(mx_out, axis=0, keepdims=True)
            acc = jnp.zeros((VT_ROWS, tq), F32)
        for c in range(n_chunks):
            k0, k1 = c * KEY_CHUNK, (c + 1) * KEY_CHUNK
            if h_in is not None:
                s = jnp.dot(ksh_ref[k0:k1, :], qp, preferred_element_type=F32)
                s_in[k0:k1, :] = s
                mx_in = jnp.maximum(mx_in, jnp.max(s.reshape(KEY_CHUNK // 8, 8, tq), axis=0))
            if h_out is not None and (c + 1) % PV_CHUNKS == 0:
                k0 = k1 - PV_CHUNKS * KEY_CHUNK
                p = jnp.exp2(s_out[k0:k1, :] - m).astype(BF16)
                acc = acc + jnp.dot(ckvt_ref[:, k0:k1], p, preferred_element_type=F32)
        if h_out is not None:
            on = acc[0:KV_LORA] * (1.0 / acc[KV_LORA:KV_LORA + 1])
            ov = jnp.dot(wvt_ref[h_out], on.astype(BF16), preferred_element_type=F32)
            o_scr[pl.ds(pl.multiple_of(h_out * V_HEAD, V_HEAD), V_HEAD), :] = ov.astype(BF16)
        return mx_in

    q0_scr[...] = make_qp(0)
    mx = stage(None, None, None, 0, s0_scr, q0_scr, 1, q1_scr)

    def head_pair(j, mx):
        mx = stage(2 * j, mx, s0_scr, 2 * j + 1, s1_scr, q1_scr, 2 * j + 2, q0_scr)
        return stage(2 * j + 1, mx, s1_scr, 2 * j + 2, s0_scr, q0_scr, 2 * j + 3, q1_scr)

    mx = lax.fori_loop(0, N_HEADS // 2 - 1, head_pair, mx)
    mx = stage(N_HEADS - 2, mx, s0_scr, N_HEADS - 1, s1_scr, q1_scr, None, None)
    stage(N_HEADS - 1, mx, s1_scr, None, None, None, None, None)
    attn = lax.dot_general(o_scr[...], wo_ref[...], _TN, preferred_element_type=F32)
    x1 = x_ref[...] + attn
    x1_ref[...] = x1
    _ffn_norm_and_route(x1, gffn_ref, wr_ref, br_ref, tri_ref, h2_ref, slab_ref, cnt_ref)


def _attn_layer(x, cqt, ksh, ckvt, cost, sint, wabst, wqrt, wvt, wo, gffn, wr, br, tri):
    batch, seq, _ = x.shape
    tq = min(SEQ_TILE, seq)
    out_shapes, out_specs = _tail_out(batch, seq, tq)
    return pl.pallas_call(
        _attn_kernel,
        grid=(batch, seq // tq),
        in_specs=[
            pl.BlockSpec((None, tq, D_MODEL), lambda b, i: (b, i, 0)),
            pl.BlockSpec((None, Q_LORA, tq), lambda b, i: (b, 0, i)),
            pl.BlockSpec((None, seq, KEY_WIDTH), lambda b, i: (b, 0, 0)),
            pl.BlockSpec((None, VT_ROWS, seq), lambda b, i: (b, 0, 0)),
            pl.BlockSpec((QK_ROPE // 2, tq), lambda b, i: (0, i)),
            pl.BlockSpec((QK_ROPE // 2, tq), lambda b, i: (0, i)),
            _const_spec((N_HEADS, KV_LORA, Q_LORA)),
            _const_spec((N_HEADS, QK_ROPE, Q_LORA)),
            _const_spec((N_HEADS, V_HEAD, KV_LORA)),
            _const_spec((N_HEADS * V_HEAD, D_MODEL)),
            _const_spec((1, D_MODEL)),
            _const_spec((D_MODEL, LANES)),
            _const_spec((1, LANES)),
            _const_spec((tq, tq)),
        ],
        out_specs=out_specs,
        out_shape=out_shapes,
        scratch_shapes=[pltpu.VMEM((N_HEADS * V_HEAD, tq), BF16),
                        pltpu.VMEM((seq, tq), F32), pltpu.VMEM((seq, tq), F32),
                        pltpu.VMEM((KEY_WIDTH, tq), BF16), pltpu.VMEM((KEY_WIDTH, tq), BF16)],
        compiler_params=pltpu.CompilerParams(
            dimension_semantics=("arbitrary", "arbitrary"), vmem_limit_bytes=VMEM_LIMIT),
        name="mla_attention",
    )(x, cqt, ksh, ckvt, cost, sint, wabst, wqrt, wvt, wo, gffn, wr, br, tri)


def _token_copy(src, src_tok, dst, dst_tok, sem):
    s = pl.multiple_of(src_tok * TOK_ROWS, TOK_ROWS)
    d = pl.multiple_of(dst_tok * TOK_ROWS, TOK_ROWS)
    return pltpu.make_async_copy(src.at[pl.ds(s, TOK_ROWS), :], dst.at[pl.ds(d, TOK_ROWS), :], sem)


def _for_each_token(ts, fn):
    def body(it, c):
        for u in range(ROW_UNROLL):
            fn(it * ROW_UNROLL + u)
        return c
    lax.fori_loop(0, ts // ROW_UNROLL, body, 0)


def _scatter_kernel(zb_ref, zv_ref, h2_hbm, dest_hbm, xs_hbm, idx_smem, stage, zeros_vmem,
                    sem_idx, sem_stage, sem_zero, sem_rows):
    i = pl.program_id(0)
    n_tiles = pl.num_programs(0)
    ts = idx_smem.shape[0] // (2 * IDX_SLOTS)
    tile_rows = ts * TOK_ROWS
    zrows = zeros_vmem.shape[0]
    n_zero = zb_ref.shape[0]

    def zero_copy(e):
        start = pl.multiple_of(zb_ref[e] * TOK_ROWS, zrows)
        return pltpu.make_async_copy(zeros_vmem, xs_hbm.at[pl.ds(start, zrows), :], sem_zero)

    @pl.when(i == 0)
    def _():
        zeros_vmem[...] = jnp.zeros_like(zeros_vmem)

        def start(e, c):
            @pl.when(zv_ref[e] > 0)
            def _():
                zero_copy(e).start()
            return c

        def wait(e, c):
            @pl.when(zv_ref[e] > 0)
            def _():
                zero_copy(e).wait()
            return c

        lax.fori_loop(0, n_zero, start, 0)
        lax.fori_loop(0, n_zero, wait, 0)

    def fetches(t):
        slot = lax.rem(t, IDX_SLOTS)
        rows = h2_hbm.at[pl.ds(pl.multiple_of(t * tile_rows, tile_rows), tile_rows), :]
        return [pltpu.make_async_copy(dest_hbm.at[t], idx_smem.at[pl.ds(slot * (2 * ts), 2 * ts)], sem_idx.at[slot]),
                pltpu.make_async_copy(rows, stage.at[slot], sem_stage.at[slot])]

    def copies(t, r):
        slot = lax.rem(t, IDX_SLOTS)
        sem = sem_rows.at[lax.rem(t, 2)]
        return [_token_copy(stage.at[slot], r, xs_hbm, idx_smem[slot * (2 * ts) + 2 * r + k], sem)
                for k in range(2)]

    def start_tile(t):
        _for_each_token(ts, lambda r: [c.start(priority=k) for k, c in enumerate(copies(t, r))])

    def wait_tile(t):
        _for_each_token(ts, lambda r: [c.wait() for c in copies(t, r)])

    @pl.when(i == 0)
    def _():
        for c in fetches(i):
            c.start()

    for c in fetches(i):
        c.wait()

    @pl.when(i + 1 < n_tiles)
    def _():
        for c in fetches(i + 1):
            c.start()

    start_tile(i)

    @pl.when(i > 0)
    def _():
        wait_tile(i - 1)

    @pl.when(i == n_tiles - 1)
    def _():
        wait_tile(i)


def _scatter_rows(h2, dest, zb, zv, n_rows, bm):
    ts = dest.shape[1] // 2
    tokens = h2.shape[0] // TOK_ROWS
    return pl.pallas_call(
        _scatter_kernel,
        grid_spec=pltpu.PrefetchScalarGridSpec(
            num_scalar_prefetch=2,
            grid=(tokens // ts,),
            in_specs=[pl.BlockSpec(memory_space=pl.ANY),
                      pl.BlockSpec(memory_space=pl.ANY)],
            out_specs=pl.BlockSpec(memory_space=pl.ANY),
            scratch_shapes=[pltpu.SMEM((IDX_SLOTS * 2 * ts,), jnp.int32),
                            pltpu.VMEM((IDX_SLOTS, ts * TOK_ROWS, LANES), F32),
                            pltpu.VMEM((bm * TOK_ROWS, LANES), F32),
                            pltpu.SemaphoreType.DMA((IDX_SLOTS,)), pltpu.SemaphoreType.DMA((IDX_SLOTS,)),
                            pltpu.SemaphoreType.DMA, pltpu.SemaphoreType.DMA((2,))],
        ),
        out_shape=jax.ShapeDtypeStruct((n_rows * TOK_ROWS, LANES), F32),
        compiler_params=pltpu.CompilerParams(
            dimension_semantics=("arbitrary",), vmem_limit_bytes=VMEM_LIMIT),
        name="moe_scatter",
    )(zb, zv, h2, dest)


def _expert_kernel(be_ref, nv_ref, xs_ref, wg_ref, wu_ref, wd_ref, ys_ref, wg_bf, wu_bf, wd_bf):
    b = pl.program_id(0)

    @pl.when(b < nv_ref[0])
    def _():
        changed = (b == 0) | (be_ref[b] != be_ref[jnp.maximum(b - 1, 0)])

        @pl.when(changed)
        def _():
            wg_bf[...] = wg_ref[...].astype(BF16)
            wu_bf[...] = wu_ref[...].astype(BF16)
            wd_bf[...] = wd_ref[...].astype(BF16)

        xb = _from_token_tiles(xs_ref, xs_ref.shape[0] // TOK_ROWS).astype(BF16)
        gate = jnp.dot(xb, wg_bf[...], preferred_element_type=F32)
        up = jnp.dot(xb, wu_bf[...], preferred_element_type=F32)
        hid = gate * (1.0 / (1.0 + jnp.exp(-gate))) * up
        _to_token_tiles(ys_ref, jnp.dot(hid.astype(BF16), wd_bf[...], preferred_element_type=F32))

    @pl.when(b >= nv_ref[0])
    def _():
        ys_ref[...] = jnp.zeros_like(ys_ref)


def _expert_mlp(xs, block_expert, n_valid, wg, wu, wd, layer, bm):
    n_blocks = xs.shape[0] // (bm * TOK_ROWS)
    blk = lambda b, be, nv: (jnp.minimum(b, nv[0] - 1), 0)
    out_blk = lambda b, be, nv: (b, 0)
    wsel = lambda b, be, nv: (layer, be[jnp.minimum(b, nv[0] - 1)], 0, 0)
    return pl.pallas_call(
        _expert_kernel,
        grid_spec=pltpu.PrefetchScalarGridSpec(
            num_scalar_prefetch=2,
            grid=(n_blocks,),
            in_specs=[pl.BlockSpec((bm * TOK_ROWS, LANES), blk),
                      pl.BlockSpec((None, None, D_MODEL, D_EXPERT), wsel),
                      pl.BlockSpec((None, None, D_MODEL, D_EXPERT), wsel),
                      pl.BlockSpec((None, None, D_EXPERT, D_MODEL), wsel)],
            out_specs=pl.BlockSpec((bm * TOK_ROWS, LANES), out_blk),
            scratch_shapes=[pltpu.VMEM((D_MODEL, D_EXPERT), BF16),
                            pltpu.VMEM((D_MODEL, D_EXPERT), BF16),
                            pltpu.VMEM((D_EXPERT, D_MODEL), BF16)],
        ),
        out_shape=jax.ShapeDtypeStruct(xs.shape, F32),
        compiler_params=pltpu.CompilerParams(
            dimension_semantics=("arbitrary",), vmem_limit_bytes=VMEM_LIMIT),
        name="moe_experts",
    )(block_expert, n_valid, xs, wg, wu, wd)


def _combine_kernel(first_tiles, x1_ref, slab_ref, dest_hbm, ys_hbm, *rest):
    if first_tiles is None:
        x2_ref, idx_smem, g0, g1, sem_idx, sem_rows = rest
    else:
        gfin_ref, out_a_ref, out_b_ref, idx_smem, g0, g1, sem_idx, sem_rows = rest
    i = pl.program_id(0)
    n_tiles = pl.num_programs(0)
    ts = x1_ref.shape[0]

    def idx_copy(t):
        slot = lax.rem(t, IDX_SLOTS)
        return pltpu.make_async_copy(dest_hbm.at[t], idx_smem.at[pl.ds(slot * (2 * ts), 2 * ts)], sem_idx.at[slot])

    def copies(t, r):
        slot = lax.rem(t, IDX_SLOTS)
        half = lax.rem(t, 2)
        return [_token_copy(ys_hbm, idx_smem[slot * (2 * ts) + 2 * r + k], g.at[half], r, sem_rows.at[half])
                for k, g in enumerate((g0, g1))]

    def start_tile(t):
        _for_each_token(ts, lambda r: [c.start(priority=k) for k, c in enumerate(copies(t, r))])

    @pl.when(i == 0)
    def _():
        idx_copy(i).start()
        idx_copy(i).wait()
        start_tile(i)

        @pl.when(n_tiles > 1)
        def _():
            idx_copy(i + 1).start()

    @pl.when(i + 1 < n_tiles)
    def _():
        idx_copy(i + 1).wait()
        start_tile(i + 1)

    @pl.when(i + 2 < n_tiles)
    def _():
        idx_copy(i + 2).start()

    _for_each_token(ts, lambda r: [c.wait() for c in copies(i, r)])
    half = lax.rem(i, 2)
    slab = slab_ref[...]
    y = (slab[:, L_W1:L_W1 + 1] * _from_token_tiles(g0.at[half], ts)
         + slab[:, L_W2:L_W2 + 1] * _from_token_tiles(g1.at[half], ts))
    x2 = x1_ref[...] + y
    if first_tiles is None:
        x2_ref[...] = x2
    else:
        out = _rms(x2) * gfin_ref[...]

        @pl.when(i < first_tiles)
        def _():
            out_a_ref[...] = out

        @pl.when(i >= first_tiles)
        def _():
            out_b_ref[...] = out


def _combine_rows(x1, slab, dest, ys, final=None):
    tokens = x1.shape[0]
    ts = dest.shape[1] // 2
    tok = lambda i: (i, 0)
    in_specs = [pl.BlockSpec((ts, D_MODEL), tok),
                pl.BlockSpec((ts, LANES), tok),
                pl.BlockSpec(memory_space=pl.ANY),
                pl.BlockSpec(memory_space=pl.ANY)]
    args = [x1, slab, dest, ys]
    if final is None:
        first_tiles = None
        out_specs = pl.BlockSpec((ts, D_MODEL), tok)
        out_shape = jax.ShapeDtypeStruct(x1.shape, F32)
    else:
        gain, first_tokens = final
        first_tiles = first_tokens // ts
        in_specs.append(pl.BlockSpec((1, D_MODEL), lambda i: (0, 0)))
        args.append(gain)
        out_specs = (pl.BlockSpec((ts, D_MODEL), lambda i: (jnp.minimum(i, first_tiles - 1), 0)),
                     pl.BlockSpec((ts, D_MODEL), lambda i: (jnp.maximum(i - first_tiles, 0), 0)))
        out_shape = (jax.ShapeDtypeStruct((first_tokens, D_MODEL), F32),
                     jax.ShapeDtypeStruct((tokens - first_tokens, D_MODEL), F32))
    return pl.pallas_call(
        functools.partial(_combine_kernel, first_tiles),
        grid=(tokens // ts,),
        in_specs=in_specs,
        out_specs=out_specs,
        out_shape=out_shape,
        scratch_shapes=[pltpu.SMEM((IDX_SLOTS * 2 * ts,), jnp.int32),
                        pltpu.VMEM((2, ts * TOK_ROWS, LANES), F32), pltpu.VMEM((2, ts * TOK_ROWS, LANES), F32),
                        pltpu.SemaphoreType.DMA((IDX_SLOTS,)), pltpu.SemaphoreType.DMA((2,))],
        compiler_params=pltpu.CompilerParams(
            dimension_semantics=("arbitrary",), vmem_limit_bytes=VMEM_LIMIT),
        name="moe_combine",
    )(*args)


def _moe(x1, h2, slab, counts, wg, wu, wd, layer, final=None):
    tokens = x1.shape[0]
    bm = MOE_BLOCK
    ts = min(ROW_TILE, tokens)
    n_blocks = -(-2 * tokens // bm) + N_EXPERTS
    cnt = counts[0, :N_EXPERTS].astype(jnp.int32)
    padded = ((cnt + bm - 1) // bm) * bm
    pend = jnp.cumsum(padded)
    pstart = pend - padded
    eid = slab[:, L_E1:L_E2 + 1].astype(jnp.int32)
    rank = slab[:, L_R1:L_R2 + 1].astype(jnp.int32)
    hit = eid[:, :, None] == jnp.arange(N_EXPERTS, dtype=jnp.int32)
    dest = (jnp.sum(jnp.where(hit, pstart, 0), axis=-1) + rank).reshape(tokens // ts, 2 * ts)
    block_start = jnp.arange(n_blocks, dtype=jnp.int32) * bm
    block_expert = jnp.minimum(jnp.sum(pend[None, :] <= block_start[:, None], axis=1), N_EXPERTS - 1)
    n_valid = (pend[-1:] // bm).astype(jnp.int32)
    trailing = n_valid[0] + jnp.arange(N_EXPERTS, dtype=jnp.int32)
    zb = jnp.concatenate([pend - bm, trailing * bm]).astype(jnp.int32)
    zv = jnp.concatenate([padded > 0, trailing < n_blocks]).astype(jnp.int32)

    xs = _scatter_rows(h2, dest, zb, zv, n_blocks * bm, bm)
    ys = _expert_mlp(xs, block_expert.astype(jnp.int32), n_valid, wg, wu, wd, layer, bm)
    return _combine_rows(x1, slab, dest, ys, final)


def _rope_tables(seq):
    inv_freq = 1.0 / (ROPE_THETA ** (jnp.arange(0, QK_ROPE, 2, dtype=F32) / QK_ROPE))
    ang = jnp.arange(seq, dtype=F32)[:, None] * inv_freq[None, :]
    return jnp.cos(ang), jnp.sin(ang)


def _trunk(inputs, norm_mix, norm_ffn, norm_final, pool_w, pool_scale,
           mla_w_dq, mla_q_norm, mla_w_uq, mla_w_dkv, mla_kv_norm, mla_w_ukv, mla_w_o,
           moe_w_group, moe_b_group, moe_w_router, moe_b_router, moe_w_gate, moe_w_up, moe_w_down):
    seq = inputs[0].shape[1]
    batch = sum(a.shape[0] for a in inputs)
    x = None
    ts = min(SEQ_TILE, seq)
    tri = jnp.tril(jnp.ones((ts, ts), F32), k=-1).astype(BF16)
    cos, sin = _rope_tables(seq)
    lane_pad = ((0, 0), (0, LANES - QK_ROPE))
    cos2 = jnp.pad(jnp.concatenate([cos, cos], axis=1), lane_pad)
    sin2 = jnp.pad(jnp.concatenate([-sin, sin], axis=1), lane_pad)
    cost, sint = cos.T, sin.T
    row = lambda v: v.reshape(1, -1)
    col = lambda v: v.reshape(-1, 1)

    for i in range(DEPTH):
        j = i // 2
        pad = LANES - N_EXPERTS - N_GROUPS
        wr = jnp.pad(jnp.concatenate([moe_w_router[i], moe_w_group[i]], axis=1), ((0, 0), (0, pad))).astype(BF16)
        br = row(jnp.pad(jnp.concatenate([moe_b_router[i], moe_b_group[i]]), (0, pad)))
        gmix, gffn = row(norm_mix[i]), row(norm_ffn[i])
        if i % 2 == 0:
            x1, h2, slab, counts = _pool_layer(inputs if x is None else [x], gmix, pool_w[j].astype(BF16),
                                               row(pool_scale[j]), gffn, wr, br, tri)
        else:
            w_uq = mla_w_uq[j].reshape(Q_LORA, N_HEADS, QK_NOPE + QK_ROPE)
            w_ukv = mla_w_ukv[j].reshape(KV_LORA, N_HEADS, QK_NOPE + V_HEAD)
            wabst = _absorb_weights(w_ukv[:, :, :QK_NOPE].transpose(1, 0, 2),
                                    w_uq[:, :, :QK_NOPE].transpose(1, 2, 0))
            wqrt = w_uq[:, :, QK_NOPE:].transpose(1, 2, 0).astype(BF16)
            wvt = w_ukv[:, :, QK_NOPE:].transpose(1, 2, 0).astype(BF16)
            w_ckv = mla_w_dkv[j][:, :KV_LORA]
            w_kr = mla_w_dkv[j][:, KV_LORA:]
            half = QK_ROPE // 2
            w_kr_swapped = jnp.concatenate([w_kr[:, half:], w_kr[:, :half]], axis=1)
            wkv = jnp.concatenate([w_ckv, jnp.pad(w_kr, lane_pad), jnp.pad(w_kr_swapped, lane_pad)],
                                  axis=1).astype(BF16)
            cqt, ksh, ckvt = _mla_proj(x, gmix, mla_w_dq[j].T.astype(BF16), col(mla_q_norm[j]), wkv,
                                       w_ckv.T.astype(BF16), row(mla_kv_norm[j]), col(mla_kv_norm[j]),
                                       cos2, sin2)
            x1, h2, slab, counts = _attn_layer(x, cqt, ksh, ckvt, cost, sint, wabst, wqrt, wvt,
                                               mla_w_o[j].astype(BF16), gffn, wr, br, tri)
        if i < DEPTH - 1:
            x = _moe(x1, h2, slab, counts, moe_w_gate, moe_w_up, moe_w_down, i).reshape(batch, seq, D_MODEL)
    outs = _moe(x1, h2, slab, counts, moe_w_gate, moe_w_up, moe_w_down, DEPTH - 1,
                final=(row(norm_final), inputs[0].shape[0] * seq))
    return tuple(o.reshape(a.shape) for o, a in zip(outs, inputs))


def kernel(x_prompt, x_sample, norm_mix, norm_ffn, norm_final, pool_w, pool_scale, mla_w_dq, mla_q_norm,
           mla_w_uq, mla_w_dkv, mla_kv_norm, mla_w_ukv, mla_w_o, moe_w_group, moe_b_group, moe_w_router,
           moe_b_router, moe_w_gate, moe_w_up, moe_w_down):
    return _trunk([x_prompt, x_sample], norm_mix, norm_ffn, norm_final, pool_w, pool_scale,
                  mla_w_dq, mla_q_norm, mla_w_uq, mla_w_dkv, mla_kv_norm, mla_w_ukv, mla_w_o,
                  moe_w_group, moe_b_group, moe_w_router, moe_b_router, moe_w_gate, moe_w_up, moe_w_down)
```

```python
import functools
import math

import jax
import jax.numpy as jnp
from jax import lax
from jax.experimental import pallas as pl
from jax.experimental.pallas import tpu as pltpu

F32 = jnp.float32
BF16 = jnp.bfloat16

D_MODEL = 1024
DEPTH = 4
POOL_WINDOWS = (2, 4, 8, 16)
POOL_GROUP = D_MODEL // len(POOL_WINDOWS)
POOL_HALO = 8
N_HEADS = 16
QK_NOPE = 64
QK_ROPE = 32
V_HEAD = 64
Q_LORA = 256
KV_LORA = 128
ROPE_THETA = 10000.0
N_GROUPS = 8
EXPERTS_PER_GROUP = 8
N_EXPERTS = N_GROUPS * EXPERTS_PER_GROUP
D_EXPERT = 512
EPS = 1e-6

LANES = 128
TOK_ROWS = D_MODEL // LANES
PACK_ROWS = TOK_ROWS // 2
ROW_UNROLL = 8
IDX_SLOTS = 3
KEY_WIDTH = 2 * LANES
VT_ROWS = KV_LORA + 16
KEY_CHUNK = 256
PV_CHUNKS = 2
SEQ_TILE = 512
ROW_TILE = 512
MOE_BLOCK = 512
VMEM_LIMIT = 48 * 1024 * 1024
SCORE_SCALE = (QK_NOPE + QK_ROPE) ** -0.5 * math.log2(math.e)

L_E1, L_E2, L_W1, L_W2, L_R1, L_R2 = range(6)


def _rms(v, axis=-1):
    return v * lax.rsqrt(jnp.mean(v * v, axis=axis, keepdims=True) + EPS)


def _to_packed_tiles(ref, v):
    n = v.shape[0]
    bits = lambda a: lax.bitcast_convert_type(a.astype(BF16).astype(F32), jnp.uint32)
    for j in range(PACK_ROWS):
        hi = bits(v[:, j * LANES:(j + 1) * LANES])
        lo = bits(v[:, D_MODEL // 2 + j * LANES:D_MODEL // 2 + (j + 1) * LANES])
        ref[pl.ds(j, n, stride=PACK_ROWS), :] = hi | (lo >> 16)


def _from_packed_tiles(ref, n):
    his, los = [], []
    for j in range(PACK_ROWS):
        w = ref[pl.ds(j, n, stride=PACK_ROWS), :]
        his.append(lax.bitcast_convert_type(w & jnp.uint32(0xFFFF0000), F32).astype(BF16))
        los.append(lax.bitcast_convert_type(w << 16, F32).astype(BF16))
    return jnp.concatenate(his + los, axis=1)


def _ffn_norm_and_route(x1, gffn_ref, wr_ref, br_ref, tri_ref, h2_ref, slab_ref, cnt_ref):
    h2 = _rms(x1) * gffn_ref[...]
    _to_packed_tiles(h2_ref, h2)
    logits = jnp.dot(h2.astype(BF16), wr_ref[...], preferred_element_type=F32) + br_ref[...]
    lane = lax.broadcasted_iota(jnp.int32, logits.shape, 1)
    lanef = lane.astype(F32)
    neg = jnp.float32(-jnp.inf)
    big = jnp.float32(1e9)

    gmask = (lane >= N_EXPERTS) & (lane < N_EXPERTS + N_GROUPS)
    gl = jnp.where(gmask, logits, neg)
    gmax = jnp.max(gl, axis=-1, keepdims=True)
    gsum = jnp.sum(jnp.exp(gl - gmax), axis=-1, keepdims=True)
    g_p = 1.0 / gsum
    gidx = jnp.min(jnp.where(gl == gmax, lanef, big), axis=-1, keepdims=True) - N_EXPERTS
    lo = gidx * EXPERTS_PER_GROUP
    emask = (lanef >= lo) & (lanef < lo + EXPERTS_PER_GROUP)
    el = jnp.where(emask, logits, neg)
    m1 = jnp.max(el, axis=-1, keepdims=True)
    i1 = jnp.min(jnp.where(el == m1, lanef, big), axis=-1, keepdims=True)
    el2 = jnp.where(lanef == i1, neg, el)
    m2 = jnp.max(el2, axis=-1, keepdims=True)
    i2 = jnp.min(jnp.where(el2 == m2, lanef, big), axis=-1, keepdims=True)
    t = jnp.exp(m2 - m1)
    w1 = g_p / (1.0 + t)
    w2 = w1 * t

    oh1 = jnp.where(lanef == i1, 1.0, 0.0)
    oh2 = jnp.where(lanef == i2, 1.0, 0.0)
    oh = oh1 + oh2
    before = jnp.dot(tri_ref[...], oh.astype(BF16), preferred_element_type=F32) + cnt_ref[...]
    r1 = jnp.sum(oh1 * before, axis=-1, keepdims=True)
    r2 = jnp.sum(oh2 * before, axis=-1, keepdims=True)
    cnt_ref[...] = cnt_ref[...] + jnp.sum(oh, axis=0, keepdims=True)

    slab = jnp.zeros_like(logits)
    for l, v in ((L_E1, i1), (L_E2, i2), (L_W1, w1), (L_W2, w2), (L_R1, r1), (L_R2, r2)):
        slab = jnp.where(lane == l, v, slab)
    slab_ref[...] = slab


def _first_step():
    return (pl.program_id(0) == 0) & (pl.program_id(1) == 0)


def _pool_kernel(seq_len, batch_ends, *refs):
    n_src = len(batch_ends)
    src_refs, refs = refs[:3 * n_src], refs[3 * n_src:]
    (gmix_ref, pw_ref, ps_ref, gffn_ref, wr_ref, br_ref, tri_ref,
     x1_ref, h2_ref, slab_ref, cnt_ref, hcat_ref) = refs
    b = pl.program_id(0)
    i = pl.program_id(1)
    ts = x1_ref.shape[0]

    def pick(which):
        v = src_refs[3 * (n_src - 1) + which][...]
        for k in range(n_src - 2, -1, -1):
            v = jnp.where(b < batch_ends[k], src_refs[3 * k + which][...], v)
        return v

    @pl.when(_first_step())
    def _():
        cnt_ref[...] = jnp.zeros_like(cnt_ref)

    g = gmix_ref[...]
    x = pick(0)
    h = _rms(x) * g
    hp = jnp.where(i > 0, _rms(pick(1)) * g, 0.0)
    hn = jnp.where(i < pl.num_programs(1) - 1, _rms(pick(2)) * g, 0.0)
    hcat_ref[0:POOL_HALO, :] = hp
    hcat_ref[POOL_HALO:POOL_HALO + ts, :] = h
    hcat_ref[POOL_HALO + ts:2 * POOL_HALO + ts, :] = hn

    pos = i * ts + lax.broadcasted_iota(jnp.int32, (ts, 1), 0)
    for gi, w in enumerate(POOL_WINDOWS):
        c0 = gi * POOL_GROUP
        acc = None
        for j in range(-(w // 2), w // 2):
            v = hcat_ref[POOL_HALO + j:POOL_HALO + j + ts, c0:c0 + POOL_GROUP]
            acc = v if acc is None else acc + v
        cnt = (jnp.minimum(pos + w // 2, seq_len) - jnp.maximum(pos - w // 2, 0)).astype(F32)
        pooled = acc / cnt - h[:, c0:c0 + POOL_GROUP]
        mixed = jnp.dot(pooled.astype(BF16), pw_ref[gi], preferred_element_type=F32)
        x1_ref[:, c0:c0 + POOL_GROUP] = x[:, c0:c0 + POOL_GROUP] + mixed * ps_ref[:, c0:c0 + POOL_GROUP]

    _ffn_norm_and_route(x1_ref[...], gffn_ref, wr_ref, br_ref, tri_ref, h2_ref, slab_ref, cnt_ref)


def _tail_out(batch, seq, ts):
    tok = lambda b, i: (b * (seq // ts) + i, 0)
    shapes = (jax.ShapeDtypeStruct((batch * seq, D_MODEL), F32),
              jax.ShapeDtypeStruct((batch * seq * PACK_ROWS, LANES), jnp.uint32),
              jax.ShapeDtypeStruct((batch * seq, LANES), F32),
              jax.ShapeDtypeStruct((1, LANES), F32))
    specs = (pl.BlockSpec((ts, D_MODEL), tok), pl.BlockSpec((ts * PACK_ROWS, LANES), tok),
             pl.BlockSpec((ts, LANES), tok), pl.BlockSpec((1, LANES), lambda b, i: (0, 0)))
    return shapes, specs


def _const_spec(shape):
    nd = len(shape)
    return pl.BlockSpec(shape, lambda b, i: (0,) * nd)


def _source_specs(first, count, ts, seq):
    hb = ts // POOL_HALO
    nh = seq // POOL_HALO

    def spec(rows, row_block):
        def index(b, i):
            mine = (b >= first) & (b < first + count)
            return (jnp.clip(b - first, 0, count - 1), jnp.where(mine, row_block(i), 0), 0)
        return pl.BlockSpec((None, rows, D_MODEL), index)

    return [spec(ts, lambda i: i),
            spec(POOL_HALO, lambda i: jnp.maximum(i * hb - 1, 0)),
            spec(POOL_HALO, lambda i: jnp.minimum((i + 1) * hb, nh - 1))]


def _pool_layer(sources, gmix, pw, ps, gffn, wr, br, tri):
    seq = sources[0].shape[1]
    ts = min(SEQ_TILE, seq)
    src_specs, src_args, ends, first = [], [], [], 0
    for src in sources:
        src_specs += _source_specs(first, src.shape[0], ts, seq)
        src_args += [src, src, src]
        first += src.shape[0]
        ends.append(first)
    batch = first
    out_shapes, out_specs = _tail_out(batch, seq, ts)
    return pl.pallas_call(
        functools.partial(_pool_kernel, seq, tuple(ends)),
        grid=(batch, seq // ts),
        in_specs=src_specs + [
            _const_spec((1, D_MODEL)),
            _const_spec((len(POOL_WINDOWS), POOL_GROUP, POOL_GROUP)),
            _const_spec((1, D_MODEL)),
            _const_spec((1, D_MODEL)),
            _const_spec((D_MODEL, LANES)),
            _const_spec((1, LANES)),
            _const_spec((ts, ts)),
        ],
        out_specs=out_specs,
        out_shape=out_shapes,
        scratch_shapes=[pltpu.VMEM((ts + 2 * POOL_HALO, D_MODEL), F32)],
        compiler_params=pltpu.CompilerParams(
            dimension_semantics=("arbitrary", "arbitrary"), vmem_limit_bytes=VMEM_LIMIT),
        name="pool_mixer",
    )(*src_args, gmix, pw, ps, gffn, wr, br, tri)


def _absorb_kernel(wk_ref, wqt_ref, out_ref):
    out_ref[...] = jnp.dot(wk_ref[...], wqt_ref[...], preferred_element_type=F32,
                           precision=lax.Precision.HIGHEST).astype(out_ref.dtype)


def _absorb_weights(wk, wqt):
    return pl.pallas_call(
        _absorb_kernel,
        grid=(N_HEADS,),
        in_specs=[pl.BlockSpec((None, KV_LORA, QK_NOPE), lambda h: (h, 0, 0)),
                  pl.BlockSpec((None, QK_NOPE, Q_LORA), lambda h: (h, 0, 0))],
        out_specs=pl.BlockSpec((None, KV_LORA, Q_LORA), lambda h: (h, 0, 0)),
        out_shape=jax.ShapeDtypeStruct((N_HEADS, KV_LORA, Q_LORA), BF16),
        name="absorb_weights",
    )(wk, wqt)


_NT = (((1,), (1,)), ((), ()))
_TN = (((0,), (0,)), ((), ()))


def _mla_proj_kernel(x_ref, gmix_ref, wdqt_ref, qn_ref, wkv_ref, wckvt_ref, kvn_row_ref, kvn_col_ref,
                     cos2_ref, sin2_ref, cqt_ref, ksh_ref, ckvt_ref):
    hb = (_rms(x_ref[...]) * gmix_ref[...]).astype(BF16)
    cqt = lax.dot_general(wdqt_ref[...], hb, _NT, preferred_element_type=F32)
    cqt_ref[...] = (_rms(cqt, axis=0) * qn_ref[...]).astype(BF16)
    kv = jnp.dot(hb, wkv_ref[...], preferred_element_type=F32)
    ckv = _rms(kv[:, 0:KV_LORA]) * kvn_row_ref[...]
    kr = kv[:, LANES:2 * LANES] * cos2_ref[...] + kv[:, 2 * LANES:3 * LANES] * sin2_ref[...]
    ksh_ref[:, 0:LANES] = ckv.astype(BF16)
    ksh_ref[:, LANES:2 * LANES] = kr.astype(BF16)
    ckvt = lax.dot_general(wckvt_ref[...], hb, _NT, preferred_element_type=F32)
    ckvt_ref[0:KV_LORA, :] = (_rms(ckvt, axis=0) * kvn_col_ref[...]).astype(BF16)
    ckvt_ref[KV_LORA:VT_ROWS, :] = jnp.ones((VT_ROWS - KV_LORA, ckvt.shape[1]), BF16)


def _mla_proj(x, gmix, wdqt, qn_col, wkv, wckvt, kvn_row, kvn_col, cos2, sin2):
    batch, seq, _ = x.shape
    ts = min(SEQ_TILE, seq)
    return pl.pallas_call(
        _mla_proj_kernel,
        grid=(batch, seq // ts),
        in_specs=[
            pl.BlockSpec((None, ts, D_MODEL), lambda b, i: (b, i, 0)),
            _const_spec((1, D_MODEL)),
            _const_spec((Q_LORA, D_MODEL)),
            _const_spec((Q_LORA, 1)),
            _const_spec((D_MODEL, 3 * LANES)),
            _const_spec((KV_LORA, D_MODEL)),
            _const_spec((1, KV_LORA)),
            _const_spec((KV_LORA, 1)),
            pl.BlockSpec((ts, LANES), lambda b, i: (i, 0)),
            pl.BlockSpec((ts, LANES), lambda b, i: (i, 0)),
        ],
        out_specs=(pl.BlockSpec((None, Q_LORA, ts), lambda b, i: (b, 0, i)),
                   pl.BlockSpec((None, ts, KEY_WIDTH), lambda b, i: (b, i, 0)),
                   pl.BlockSpec((None, VT_ROWS, ts), lambda b, i: (b, 0, i))),
        out_shape=(jax.ShapeDtypeStruct((batch, Q_LORA, seq), BF16),
                   jax.ShapeDtypeStruct((batch, seq, KEY_WIDTH), BF16),
                   jax.ShapeDtypeStruct((batch, VT_ROWS, seq), BF16)),
        compiler_params=pltpu.CompilerParams(
            dimension_semantics=("arbitrary", "arbitrary"), vmem_limit_bytes=VMEM_LIMIT),
        name="mla_proj",
    )(x, gmix, wdqt, qn_col, wkv, wckvt, kvn_row, kvn_col, cos2, sin2)


def _attn_kernel(x_ref, cqt_ref, ksh_ref, ckvt_ref, cost_ref, sint_ref, wabst_ref, wqrt_ref, wvt_ref, wo_ref,
                 gffn_ref, wr_ref, br_ref, tri_ref, x1_ref, h2_ref, slab_ref, cnt_ref,
                 o_scr, s0_scr, s1_scr, q0_scr, q1_scr):
    tq = x_ref.shape[0]
    half = QK_ROPE // 2

    @pl.when(_first_step())
    def _():
        cnt_ref[...] = jnp.zeros_like(cnt_ref)

    cq = cqt_ref[...]
    cost = cost_ref[...]
    sint = sint_ref[...]
    pad = jnp.zeros((KEY_WIDTH - KV_LORA - QK_ROPE, tq), F32)

    n_chunks = ksh_ref.shape[0] // KEY_CHUNK

    def make_qp(h):
        qn = jnp.dot(wabst_ref[h], cq, preferred_element_type=F32)
        qr = jnp.dot(wqrt_ref[h], cq, preferred_element_type=F32)
        a, b = qr[0:half], qr[half:QK_ROPE]
        qp = jnp.concatenate([qn, a * cost - b * sint, b * cost + a * sint, pad], axis=0)
        return (qp * SCORE_SCALE).astype(BF16)

    def stage(h_out, mx_out, s_out, h_in, s_in, q_in, h_next, q_next):
        if h_next is not None:
            q_next[...] = make_qp(h_next)
        if h_in is not None:
            qp = q_in[...]
            mx_in = jnp.full((8, tq), -jnp.inf, F32)
        else:
            mx_in = None
        if h_out is not None:
            m = jnp.max(mx_out, axis=0, keepdims=True)
            acc = jnp.zeros((VT_ROWS, tq), F32)
        for c in range(n_chunks):
            k0, k1 = c * KEY_CHUNK, (c + 1) * KEY_CHUNK
            if h_in is not None:
                s = jnp.dot(ksh_ref[k0:k1, :], qp, preferred_element_type=F32)
                s_in[k0:k1, :] = s
                mx_in = jnp.maximum(mx_in, jnp.max(s.reshape(KEY_CHUNK // 8, 8, tq), axis=0))
            if h_out is not None and (c + 1) % PV_CHUNKS == 0:
                k0 = k1 - PV_CHUNKS * KEY_CHUNK
                p = jnp.exp2(s_out[k0:k1, :] - m).astype(BF16)
                acc = acc + jnp.dot(ckvt_ref[:, k0:k1], p, preferred_element_type=F32)
        if h_out is not None:
            on = acc[0:KV_LORA] * (1.0 / acc[KV_LORA:KV_LORA + 1])
            ov = jnp.dot(wvt_ref[h_out], on.astype(BF16), preferred_element_type=F32)
            o_scr[pl.ds(pl.multiple_of(h_out * V_HEAD, V_HEAD), V_HEAD), :] = ov.astype(BF16)
        return mx_in

    q0_scr[...] = make_qp(0)
    mx = stage(None, None, None, 0, s0_scr, q0_scr, 1, q1_scr)

    def head_pair(j, mx):
        mx = stage(2 * j, mx, s0_scr, 2 * j + 1, s1_scr, q1_scr, 2 * j + 2, q0_scr)
        return stage(2 * j + 1, mx, s1_scr, 2 * j + 2, s0_scr, q0_scr, 2 * j + 3, q1_scr)

    mx = lax.fori_loop(0, N_HEADS // 2 - 1, head_pair, mx)
    mx = stage(N_HEADS - 2, mx, s0_scr, N_HEADS - 1, s1_scr, q1_scr, None, None)
    stage(N_HEADS - 1, mx, s1_scr, None, None, None, None, None)
    attn = lax.dot_general(o_scr[...], wo_ref[...], _TN, preferred_element_type=F32)
    x1 = x_ref[...] + attn
    x1_ref[...] = x1
    _ffn_norm_and_route(x1, gffn_ref, wr_ref, br_ref, tri_ref, h2_ref, slab_ref, cnt_ref)


def _attn_layer(x, cqt, ksh, ckvt, cost, sint, wabst, wqrt, wvt, wo, gffn, wr, br, tri):
    batch, seq, _ = x.shape
    tq = min(SEQ_TILE, seq)
    out_shapes, out_specs = _tail_out(batch, seq, tq)
    return pl.pallas_call(
        _attn_kernel,
        grid=(batch, seq // tq),
        in_specs=[
            pl.BlockSpec((None, tq, D_MODEL), lambda b, i: (b, i, 0)),
            pl.BlockSpec((None, Q_LORA, tq), lambda b, i: (b, 0, i)),
            pl.BlockSpec((None, seq, KEY_WIDTH), lambda b, i: (b, 0, 0)),
            pl.BlockSpec((None, VT_ROWS, seq), lambda b, i: (b, 0, 0)),
            pl.BlockSpec((QK_ROPE // 2, tq), lambda b, i: (0, i)),
            pl.BlockSpec((QK_ROPE // 2, tq), lambda b, i: (0, i)),
            _const_spec((N_HEADS, KV_LORA, Q_LORA)),
            _const_spec((N_HEADS, QK_ROPE, Q_LORA)),
            _const_spec((N_HEADS, V_HEAD, KV_LORA)),
            _const_spec((N_HEADS * V_HEAD, D_MODEL)),
            _const_spec((1, D_MODEL)),
            _const_spec((D_MODEL, LANES)),
            _const_spec((1, LANES)),
            _const_spec((tq, tq)),
        ],
        out_specs=out_specs,
        out_shape=out_shapes,
        scratch_shapes=[pltpu.VMEM((N_HEADS * V_HEAD, tq), BF16),
                        pltpu.VMEM((seq, tq), F32), pltpu.VMEM((seq, tq), F32),
                        pltpu.VMEM((KEY_WIDTH, tq), BF16), pltpu.VMEM((KEY_WIDTH, tq), BF16)],
        compiler_params=pltpu.CompilerParams(
            dimension_semantics=("arbitrary", "arbitrary"), vmem_limit_bytes=VMEM_LIMIT),
        name="mla_attention",
    )(x, cqt, ksh, ckvt, cost, sint, wabst, wqrt, wvt, wo, gffn, wr, br, tri)


def _token_copy(src, src_tok, dst, dst_tok, sem):
    s = pl.multiple_of(src_tok * PACK_ROWS, PACK_ROWS)
    d = pl.multiple_of(dst_tok * PACK_ROWS, PACK_ROWS)
    return pltpu.make_async_copy(src.at[pl.ds(s, PACK_ROWS), :], dst.at[pl.ds(d, PACK_ROWS), :], sem)


def _for_each_token(ts, fn):
    def body(it, c):
        for u in range(ROW_UNROLL):
            fn(it * ROW_UNROLL + u)
        return c
    lax.fori_loop(0, ts // ROW_UNROLL, body, 0)


def _scatter_kernel(zb_ref, zv_ref, h2_hbm, dest_hbm, xs_hbm, idx_smem, stage, zeros_vmem,
                    sem_idx, sem_stage, sem_zero, sem_rows):
    i = pl.program_id(0)
    n_tiles = pl.num_programs(0)
    ts = idx_smem.shape[0] // (2 * IDX_SLOTS)
    tile_rows = ts * PACK_ROWS
    zrows = zeros_vmem.shape[0]
    n_zero = zb_ref.shape[0]

    def zero_copy(e):
        start = pl.multiple_of(zb_ref[e] * PACK_ROWS, zrows)
        return pltpu.make_async_copy(zeros_vmem, xs_hbm.at[pl.ds(start, zrows), :], sem_zero)

    @pl.when(i == 0)
    def _():
        zeros_vmem[...] = jnp.zeros_like(zeros_vmem)

        def start(e, c):
            @pl.when(zv_ref[e] > 0)
            def _():
                zero_copy(e).start()
            return c

        def wait(e, c):
            @pl.when(zv_ref[e] > 0)
            def _():
                zero_copy(e).wait()
            return c

        lax.fori_loop(0, n_zero, start, 0)
        lax.fori_loop(0, n_zero, wait, 0)

    def fetches(t):
        slot = lax.rem(t, IDX_SLOTS)
        rows = h2_hbm.at[pl.ds(pl.multiple_of(t * tile_rows, tile_rows), tile_rows), :]
        return [pltpu.make_async_copy(dest_hbm.at[t], idx_smem.at[pl.ds(slot * (2 * ts), 2 * ts)], sem_idx.at[slot]),
                pltpu.make_async_copy(rows, stage.at[slot], sem_stage.at[slot])]

    def copies(t, r):
        slot = lax.rem(t, IDX_SLOTS)
        sem = sem_rows.at[lax.rem(t, 2)]
        return [_token_copy(stage.at[slot], r, xs_hbm, idx_smem[slot * (2 * ts) + 2 * r + k], sem)
                for k in range(2)]

    def start_tile(t):
        _for_each_token(ts, lambda r: [c.start(priority=k) for k, c in enumerate(copies(t, r))])

    def wait_tile(t):
        _for_each_token(ts, lambda r: [c.wait() for c in copies(t, r)])

    @pl.when(i == 0)
    def _():
        for c in fetches(i):
            c.start()

    for c in fetches(i):
        c.wait()

    @pl.when(i + 1 < n_tiles)
    def _():
        for c in fetches(i + 1):
            c.start()

    start_tile(i)

    @pl.when(i > 0)
    def _():
        wait_tile(i - 1)

    @pl.when(i == n_tiles - 1)
    def _():
        wait_tile(i)


def _scatter_rows(h2, dest, zb, zv, n_rows, bm):
    ts = dest.shape[1] // 2
    tokens = h2.shape[0] // PACK_ROWS
    return pl.pallas_call(
        _scatter_kernel,
        grid_spec=pltpu.PrefetchScalarGridSpec(
            num_scalar_prefetch=2,
            grid=(tokens // ts,),
            in_specs=[pl.BlockSpec(memory_space=pl.ANY),
                      pl.BlockSpec(memory_space=pl.ANY)],
            out_specs=pl.BlockSpec(memory_space=pl.ANY),
            scratch_shapes=[pltpu.SMEM((IDX_SLOTS * 2 * ts,), jnp.int32),
                            pltpu.VMEM((IDX_SLOTS, ts * PACK_ROWS, LANES), jnp.uint32),
                            pltpu.VMEM((bm * PACK_ROWS, LANES), jnp.uint32),
                            pltpu.SemaphoreType.DMA((IDX_SLOTS,)), pltpu.SemaphoreType.DMA((IDX_SLOTS,)),
                            pltpu.SemaphoreType.DMA, pltpu.SemaphoreType.DMA((2,))],
        ),
        out_shape=jax.ShapeDtypeStruct((n_rows * PACK_ROWS, LANES), jnp.uint32),
        compiler_params=pltpu.CompilerParams(
            dimension_semantics=("arbitrary",), vmem_limit_bytes=VMEM_LIMIT),
        name="moe_scatter",
    )(zb, zv, h2, dest)


def _expert_kernel(be_ref, nv_ref, xs_ref, wg_ref, wu_ref, wd_ref, ys_ref, wg_bf, wu_bf, wd_bf):
    b = pl.program_id(0)

    @pl.when(b < nv_ref[0])
    def _():
        changed = (b == 0) | (be_ref[b] != be_ref[jnp.maximum(b - 1, 0)])

        @pl.when(changed)
        def _():
            wg_bf[...] = wg_ref[...].astype(BF16)
            wu_bf[...] = wu_ref[...].astype(BF16)
            wd_bf[...] = wd_ref[...].astype(BF16)

        xb = _from_packed_tiles(xs_ref, xs_ref.shape[0] // PACK_ROWS)
        gate = jnp.dot(xb, wg_bf[...], preferred_element_type=F32)
        up = jnp.dot(xb, wu_bf[...], preferred_element_type=F32)
        hid = gate * (1.0 / (1.0 + jnp.exp(-gate))) * up
        _to_packed_tiles(ys_ref, jnp.dot(hid.astype(BF16), wd_bf[...], preferred_element_type=F32))

    @pl.when(b >= nv_ref[0])
    def _():
        ys_ref[...] = jnp.zeros_like(ys_ref)


def _expert_mlp(xs, block_expert, n_valid, wg, wu, wd, layer, bm):
    n_blocks = xs.shape[0] // (bm * PACK_ROWS)
    blk = lambda b, be, nv: (jnp.minimum(b, nv[0] - 1), 0)
    out_blk = lambda b, be, nv: (b, 0)
    wsel = lambda b, be, nv: (layer, be[jnp.minimum(b, nv[0] - 1)], 0, 0)
    return pl.pallas_call(
        _expert_kernel,
        grid_spec=pltpu.PrefetchScalarGridSpec(
            num_scalar_prefetch=2,
            grid=(n_blocks,),
            in_specs=[pl.BlockSpec((bm * PACK_ROWS, LANES), blk),
                      pl.BlockSpec((None, None, D_MODEL, D_EXPERT), wsel),
                      pl.BlockSpec((None, None, D_MODEL, D_EXPERT), wsel),
                      pl.BlockSpec((None, None, D_EXPERT, D_MODEL), wsel)],
            out_specs=pl.BlockSpec((bm * PACK_ROWS, LANES), out_blk),
            scratch_shapes=[pltpu.VMEM((D_MODEL, D_EXPERT), BF16),
                            pltpu.VMEM((D_MODEL, D_EXPERT), BF16),
                            pltpu.VMEM((D_EXPERT, D_MODEL), BF16)],
        ),
        out_shape=jax.ShapeDtypeStruct(xs.shape, jnp.uint32),
        compiler_params=pltpu.CompilerParams(
            dimension_semantics=("arbitrary",), vmem_limit_bytes=VMEM_LIMIT),
        name="moe_experts",
    )(block_expert, n_valid, xs, wg, wu, wd)


def _combine_kernel(first_tiles, x1_ref, slab_ref, dest_hbm, ys_hbm, *rest):
    if first_tiles is None:
        x2_ref, idx_smem, g0, g1, sem_idx, sem_rows = rest
    else:
        gfin_ref, out_a_ref, out_b_ref, idx_smem, g0, g1, sem_idx, sem_rows = rest
    i = pl.program_id(0)
    n_tiles = pl.num_programs(0)
    ts = x1_ref.shape[0]

    def idx_copy(t):
        slot = lax.rem(t, IDX_SLOTS)
        return pltpu.make_async_copy(dest_hbm.at[t], idx_smem.at[pl.ds(slot * (2 * ts), 2 * ts)], sem_idx.at[slot])

    def copies(t, r):
        slot = lax.rem(t, IDX_SLOTS)
        half = lax.rem(t, 2)
        return [_token_copy(ys_hbm, idx_smem[slot * (2 * ts) + 2 * r + k], g.at[half], r, sem_rows.at[half])
                for k, g in enumerate((g0, g1))]

    def start_tile(t):
        _for_each_token(ts, lambda r: [c.start(priority=k) for k, c in enumerate(copies(t, r))])

    @pl.when(i == 0)
    def _():
        idx_copy(i).start()
        idx_copy(i).wait()
        start_tile(i)

        @pl.when(n_tiles > 1)
        def _():
            idx_copy(i + 1).start()

    @pl.when(i + 1 < n_tiles)
    def _():
        idx_copy(i + 1).wait()
        start_tile(i + 1)

    @pl.when(i + 2 < n_tiles)
    def _():
        idx_copy(i + 2).start()

    _for_each_token(ts, lambda r: [c.wait() for c in copies(i, r)])
    half = lax.rem(i, 2)
    slab = slab_ref[...]
    y = (slab[:, L_W1:L_W1 + 1] * _from_packed_tiles(g0.at[half], ts).astype(F32)
         + slab[:, L_W2:L_W2 + 1] * _from_packed_tiles(g1.at[half], ts).astype(F32))
    x2 = x1_ref[...] + y
    if first_tiles is None:
        x2_ref[...] = x2
    else:
        out = _rms(x2) * gfin_ref[...]

        @pl.when(i < first_tiles)
        def _():
            out_a_ref[...] = out

        @pl.when(i >= first_tiles)
        def _():
            out_b_ref[...] = out


def _combine_rows(x1, slab, dest, ys, final=None):
    tokens = x1.shape[0]
    ts = dest.shape[1] // 2
    tok = lambda i: (i, 0)
    in_specs = [pl.BlockSpec((ts, D_MODEL), tok),
                pl.BlockSpec((ts, LANES), tok),
                pl.BlockSpec(memory_space=pl.ANY),
                pl.BlockSpec(memory_space=pl.ANY)]
    args = [x1, slab, dest, ys]
    if final is None:
        first_tiles = None
        out_specs = pl.BlockSpec((ts, D_MODEL), tok)
        out_shape = jax.ShapeDtypeStruct(x1.shape, F32)
    else:
        gain, first_tokens = final
        first_tiles = first_tokens // ts
        in_specs.append(pl.BlockSpec((1, D_MODEL), lambda i: (0, 0)))
        args.append(gain)
        out_specs = (pl.BlockSpec((ts, D_MODEL), lambda i: (jnp.minimum(i, first_tiles - 1), 0)),
                     pl.BlockSpec((ts, D_MODEL), lambda i: (jnp.maximum(i - first_tiles, 0), 0)))
        out_shape = (jax.ShapeDtypeStruct((first_tokens, D_MODEL), F32),
                     jax.ShapeDtypeStruct((tokens - first_tokens, D_MODEL), F32))
    return pl.pallas_call(
        functools.partial(_combine_kernel, first_tiles),
        grid=(tokens // ts,),
        in_specs=in_specs,
        out_specs=out_specs,
        out_shape=out_shape,
        scratch_shapes=[pltpu.SMEM((IDX_SLOTS * 2 * ts,), jnp.int32),
                        pltpu.VMEM((2, ts * PACK_ROWS, LANES), jnp.uint32),
                        pltpu.VMEM((2, ts * PACK_ROWS, LANES), jnp.uint32),
                        pltpu.SemaphoreType.DMA((IDX_SLOTS,)), pltpu.SemaphoreType.DMA((2,))],
        compiler_params=pltpu.CompilerParams(
            dimension_semantics=("arbitrary",), vmem_limit_bytes=VMEM_LIMIT),
        name="moe_combine",
    )(*args)


def _moe(x1, h2, slab, counts, wg, wu, wd, layer, final=None):
    tokens = x1.shape[0]
    bm = MOE_BLOCK
    ts = min(ROW_TILE, tokens)
    n_blocks = -(-2 * tokens // bm) + N_EXPERTS
    cnt = counts[0, :N_EXPERTS].astype(jnp.int32)
    padded = ((cnt + bm - 1) // bm) * bm
    pend = jnp.cumsum(padded)
    pstart = pend - padded
    eid = slab[:, L_E1:L_E2 + 1].astype(jnp.int32)
    rank = slab[:, L_R1:L_R2 + 1].astype(jnp.int32)
    hit = eid[:, :, None] == jnp.arange(N_EXPERTS, dtype=jnp.int32)
    dest = (jnp.sum(jnp.where(hit, pstart, 0), axis=-1) + rank).reshape(tokens // ts, 2 * ts)
    block_start = jnp.arange(n_blocks, dtype=jnp.int32) * bm
    block_expert = jnp.minimum(jnp.sum(pend[None, :] <= block_start[:, None], axis=1), N_EXPERTS - 1)
    n_valid = (pend[-1:] // bm).astype(jnp.int32)
    trailing = n_valid[0] + jnp.arange(N_EXPERTS, dtype=jnp.int32)
    zb = jnp.concatenate([pend - bm, trailing * bm]).astype(jnp.int32)
    zv = jnp.concatenate([padded > 0, trailing < n_blocks]).astype(jnp.int32)

    xs = _scatter_rows(h2, dest, zb, zv, n_blocks * bm, bm)
    ys = _expert_mlp(xs, block_expert.astype(jnp.int32), n_valid, wg, wu, wd, layer, bm)
    return _combine_rows(x1, slab, dest, ys, final)


def _rope_tables(seq):
    inv_freq = 1.0 / (ROPE_THETA ** (jnp.arange(0, QK_ROPE, 2, dtype=F32) / QK_ROPE))
    ang = jnp.arange(seq, dtype=F32)[:, None] * inv_freq[None, :]
    return jnp.cos(ang), jnp.sin(ang)


def _trunk(inputs, norm_mix, norm_ffn, norm_final, pool_w, pool_scale,
           mla_w_dq, mla_q_norm, mla_w_uq, mla_w_dkv, mla_kv_norm, mla_w_ukv, mla_w_o,
           moe_w_group, moe_b_group, moe_w_router, moe_b_router, moe_w_gate, moe_w_up, moe_w_down):
    seq = inputs[0].shape[1]
    batch = sum(a.shape[0] for a in inputs)
    x = None
    ts = min(SEQ_TILE, seq)
    tri = jnp.tril(jnp.ones((ts, ts), F32), k=-1).astype(BF16)
    cos, sin = _rope_tables(seq)
    lane_pad = ((0, 0), (0, LANES - QK_ROPE))
    cos2 = jnp.pad(jnp.concatenate([cos, cos], axis=1), lane_pad)
    sin2 = jnp.pad(jnp.concatenate([-sin, sin], axis=1), lane_pad)
    cost, sint = cos.T, sin.T
    row = lambda v: v.reshape(1, -1)
    col = lambda v: v.reshape(-1, 1)

    for i in range(DEPTH):
        j = i // 2
        pad = LANES - N_EXPERTS - N_GROUPS
        wr = jnp.pad(jnp.concatenate([moe_w_router[i], moe_w_group[i]], axis=1), ((0, 0), (0, pad))).astype(BF16)
        br = row(jnp.pad(jnp.concatenate([moe_b_router[i], moe_b_group[i]]), (0, pad)))
        gmix, gffn = row(norm_mix[i]), row(norm_ffn[i])
        if i % 2 == 0:
            x1, h2, slab, counts = _pool_layer(inputs if x is None else [x], gmix, pool_w[j].astype(BF16),
                                               row(pool_scale[j]), gffn, wr, br, tri)
        else:
            w_uq = mla_w_uq[j].reshape(Q_LORA, N_HEADS, QK_NOPE + QK_ROPE)
            w_ukv = mla_w_ukv[j].reshape(KV_LORA, N_HEADS, QK_NOPE + V_HEAD)
            wabst = _absorb_weights(w_ukv[:, :, :QK_NOPE].transpose(1, 0, 2),
                                    w_uq[:, :, :QK_NOPE].transpose(1, 2, 0))
            wqrt = w_uq[:, :, QK_NOPE:].transpose(1, 2, 0).astype(BF16)
            wvt = w_ukv[:, :, QK_NOPE:].transpose(1, 2, 0).astype(BF16)
            w_ckv = mla_w_dkv[j][:, :KV_LORA]
            w_kr = mla_w_dkv[j][:, KV_LORA:]
            half = QK_ROPE // 2
            w_kr_swapped = jnp.concatenate([w_kr[:, half:], w_kr[:, :half]], axis=1)
            wkv = jnp.concatenate([w_ckv, jnp.pad(w_kr, lane_pad), jnp.pad(w_kr_swapped, lane_pad)],
                                  axis=1).astype(BF16)
            cqt, ksh, ckvt = _mla_proj(x, gmix, mla_w_dq[j].T.astype(BF16), col(mla_q_norm[j]), wkv,
                                       w_ckv.T.astype(BF16), row(mla_kv_norm[j]), col(mla_kv_norm[j]),
                                       cos2, sin2)
            x1, h2, slab, counts = _attn_layer(x, cqt, ksh, ckvt, cost, sint, wabst, wqrt, wvt,
                                               mla_w_o[j].astype(BF16), gffn, wr, br, tri)
        if i < DEPTH - 1:
            x = _moe(x1, h2, slab, counts, moe_w_gate, moe_w_up, moe_w_down, i).reshape(batch, seq, D_MODEL)
    outs = _moe(x1, h2, slab, counts, moe_w_gate, moe_w_up, moe_w_down, DEPTH - 1,
                final=(row(norm_final), inputs[0].shape[0] * seq))
    return tuple(o.reshape(a.shape) for o, a in zip(outs, inputs))


def kernel(x_prompt, x_sample, norm_mix, norm_ffn, norm_final, pool_w, pool_scale, mla_w_dq, mla_q_norm,
           mla_w_uq, mla_w_dkv, mla_kv_norm, mla_w_ukv, mla_w_o, moe_w_group, moe_b_group, moe_w_router,
           moe_b_router, moe_w_gate, moe_w_up, moe_w_down):
    return _trunk([x_prompt, x_sample], norm_mix, norm_ffn, norm_final, pool_w, pool_scale,
                  mla_w_dq, mla_q_norm, mla_w_uq, mla_w_dkv, mla_kv_norm, mla_w_ukv, mla_w_o,
                  moe_w_group, moe_b_group, moe_w_router, moe_b_router, moe_w_gate, moe_w_up, moe_w_down)
```

```python
import functools
import math

import jax
import jax.numpy as jnp
from jax import lax
from jax.experimental import pallas as pl
from jax.experimental.pallas import tpu as pltpu

F32 = jnp.float32
BF16 = jnp.bfloat16

D_MODEL = 1024
DEPTH = 4
POOL_WINDOWS = (2, 4, 8, 16)
POOL_GROUP = D_MODEL // len(POOL_WINDOWS)
POOL_HALO = 8
N_HEADS = 16
QK_NOPE = 64
QK_ROPE = 32
V_HEAD = 64
Q_LORA = 256
KV_LORA = 128
ROPE_THETA = 10000.0
N_GROUPS = 8
EXPERTS_PER_GROUP = 8
N_EXPERTS = N_GROUPS * EXPERTS_PER_GROUP
D_EXPERT = 512
EPS = 1e-6

LANES = 128
TOK_ROWS = D_MODEL // LANES
PACK_ROWS = TOK_ROWS // 2
ROW_UNROLL = 32
IDX_SLOTS = 3
KEY_WIDTH = 2 * LANES
VT_ROWS = KV_LORA + 16
KEY_CHUNK = 256
PV_CHUNKS = 2
SEQ_TILE = 512
ROW_TILE = 512
MOE_BLOCK = 512
VMEM_LIMIT = 48 * 1024 * 1024
SCORE_SCALE = (QK_NOPE + QK_ROPE) ** -0.5 * math.log2(math.e)

L_E1, L_E2, L_W1, L_W2, L_R1, L_R2 = range(6)


def _rms(v, axis=-1):
    return v * lax.rsqrt(jnp.mean(v * v, axis=axis, keepdims=True) + EPS)


def _to_packed_tiles(ref, v):
    n = v.shape[0]
    bits = lambda a: lax.bitcast_convert_type(a.astype(BF16).astype(F32), jnp.uint32)
    for j in range(PACK_ROWS):
        hi = bits(v[:, j * LANES:(j + 1) * LANES])
        lo = bits(v[:, D_MODEL // 2 + j * LANES:D_MODEL // 2 + (j + 1) * LANES])
        ref[pl.ds(j, n, stride=PACK_ROWS), :] = hi | (lo >> 16)


def _from_packed_tiles(ref, n):
    his, los = [], []
    for j in range(PACK_ROWS):
        w = ref[pl.ds(j, n, stride=PACK_ROWS), :]
        his.append(lax.bitcast_convert_type(w & jnp.uint32(0xFFFF0000), F32).astype(BF16))
        los.append(lax.bitcast_convert_type(w << 16, F32).astype(BF16))
    return jnp.concatenate(his + los, axis=1)


def _ffn_norm_and_route(x1, gffn_ref, wr_ref, br_ref, tri_ref, h2_ref, slab_ref, cnt_ref):
    h2 = _rms(x1) * gffn_ref[...]
    _to_packed_tiles(h2_ref, h2)
    logits = jnp.dot(h2.astype(BF16), wr_ref[...], preferred_element_type=F32) + br_ref[...]
    lane = lax.broadcasted_iota(jnp.int32, logits.shape, 1)
    lanef = lane.astype(F32)
    neg = jnp.float32(-jnp.inf)
    big = jnp.float32(1e9)

    gmask = (lane >= N_EXPERTS) & (lane < N_EXPERTS + N_GROUPS)
    gl = jnp.where(gmask, logits, neg)
    gmax = jnp.max(gl, axis=-1, keepdims=True)
    gsum = jnp.sum(jnp.exp(gl - gmax), axis=-1, keepdims=True)
    g_p = 1.0 / gsum
    gidx = jnp.min(jnp.where(gl == gmax, lanef, big), axis=-1, keepdims=True) - N_EXPERTS
    lo = gidx * EXPERTS_PER_GROUP
    emask = (lanef >= lo) & (lanef < lo + EXPERTS_PER_GROUP)
    el = jnp.where(emask, logits, neg)
    m1 = jnp.max(el, axis=-1, keepdims=True)
    i1 = jnp.min(jnp.where(el == m1, lanef, big), axis=-1, keepdims=True)
    el2 = jnp.where(lanef == i1, neg, el)
    m2 = jnp.max(el2, axis=-1, keepdims=True)
    i2 = jnp.min(jnp.where(el2 == m2, lanef, big), axis=-1, keepdims=True)
    t = jnp.exp(m2 - m1)
    w1 = g_p / (1.0 + t)
    w2 = w1 * t

    oh1 = jnp.where(lanef == i1, 1.0, 0.0)
    oh2 = jnp.where(lanef == i2, 1.0, 0.0)
    oh = oh1 + oh2
    before = jnp.dot(tri_ref[...], oh.astype(BF16), preferred_element_type=F32) + cnt_ref[...]
    r1 = jnp.sum(oh1 * before, axis=-1, keepdims=True)
    r2 = jnp.sum(oh2 * before, axis=-1, keepdims=True)
    cnt_ref[...] = cnt_ref[...] + jnp.sum(oh, axis=0, keepdims=True)

    slab = jnp.zeros_like(logits)
    for l, v in ((L_E1, i1), (L_E2, i2), (L_W1, w1), (L_W2, w2), (L_R1, r1), (L_R2, r2)):
        slab = jnp.where(lane == l, v, slab)
    slab_ref[...] = slab


def _first_step():
    return (pl.program_id(0) == 0) & (pl.program_id(1) == 0)


def _pool_kernel(batch_ends, *refs):
    n_src = len(batch_ends)
    src_refs, refs = refs[:3 * n_src], refs[3 * n_src:]
    (gmix_ref, inv_ref, pw_ref, ps_ref, gffn_ref, wr_ref, br_ref, tri_ref,
     x1_ref, h2_ref, slab_ref, cnt_ref, hcat_ref) = refs
    b = pl.program_id(0)
    i = pl.program_id(1)
    ts = x1_ref.shape[0]

    def pick(which):
        v = src_refs[3 * (n_src - 1) + which][...]
        for k in range(n_src - 2, -1, -1):
            v = jnp.where(b < batch_ends[k], src_refs[3 * k + which][...], v)
        return v

    @pl.when(_first_step())
    def _():
        cnt_ref[...] = jnp.zeros_like(cnt_ref)

    g = gmix_ref[...]
    x = pick(0)
    h = _rms(x) * g
    hp = jnp.where(i > 0, _rms(pick(1)) * g, 0.0)
    hn = jnp.where(i < pl.num_programs(1) - 1, _rms(pick(2)) * g, 0.0)
    hcat_ref[0:POOL_HALO, :] = hp
    hcat_ref[POOL_HALO:POOL_HALO + ts, :] = h
    hcat_ref[POOL_HALO + ts:2 * POOL_HALO + ts, :] = hn

    inv_cnt = inv_ref[...]
    for gi, w in enumerate(POOL_WINDOWS):
        c0 = gi * POOL_GROUP
        acc = None
        for j in range(-(w // 2), w // 2):
            v = hcat_ref[POOL_HALO + j:POOL_HALO + j + ts, c0:c0 + POOL_GROUP]
            acc = v if acc is None else acc + v
        pooled = acc * inv_cnt[:, gi:gi + 1] - h[:, c0:c0 + POOL_GROUP]
        mixed = jnp.dot(pooled.astype(BF16), pw_ref[gi], preferred_element_type=F32)
        x1_ref[:, c0:c0 + POOL_GROUP] = x[:, c0:c0 + POOL_GROUP] + mixed * ps_ref[:, c0:c0 + POOL_GROUP]

    _ffn_norm_and_route(x1_ref[...], gffn_ref, wr_ref, br_ref, tri_ref, h2_ref, slab_ref, cnt_ref)


def _tail_out(batch, seq, ts):
    tok = lambda b, i: (b * (seq // ts) + i, 0)
    shapes = (jax.ShapeDtypeStruct((batch * seq, D_MODEL), F32),
              jax.ShapeDtypeStruct((batch * seq * PACK_ROWS, LANES), jnp.uint32),
              jax.ShapeDtypeStruct((batch * seq, LANES), F32),
              jax.ShapeDtypeStruct((1, LANES), F32))
    specs = (pl.BlockSpec((ts, D_MODEL), tok), pl.BlockSpec((ts * PACK_ROWS, LANES), tok),
             pl.BlockSpec((ts, LANES), tok), pl.BlockSpec((1, LANES), lambda b, i: (0, 0)))
    return shapes, specs


def _const_spec(shape):
    nd = len(shape)
    return pl.BlockSpec(shape, lambda b, i: (0,) * nd)


def _source_specs(first, count, ts, seq):
    hb = ts // POOL_HALO
    nh = seq // POOL_HALO

    def spec(rows, row_block):
        def index(b, i):
            mine = (b >= first) & (b < first + count)
            return (jnp.clip(b - first, 0, count - 1), jnp.where(mine, row_block(i), 0), 0)
        return pl.BlockSpec((None, rows, D_MODEL), index)

    return [spec(ts, lambda i: i),
            spec(POOL_HALO, lambda i: jnp.maximum(i * hb - 1, 0)),
            spec(POOL_HALO, lambda i: jnp.minimum((i + 1) * hb, nh - 1))]


def _pool_layer(sources, gmix, pw, ps, gffn, wr, br, tri):
    seq = sources[0].shape[1]
    ts = min(SEQ_TILE, seq)
    src_specs, src_args, ends, first = [], [], [], 0
    for src in sources:
        src_specs += _source_specs(first, src.shape[0], ts, seq)
        src_args += [src, src, src]
        first += src.shape[0]
        ends.append(first)
    batch = first
    out_shapes, out_specs = _tail_out(batch, seq, ts)
    pos = jnp.arange(seq)
    inv_cnt = jnp.stack([1.0 / (jnp.minimum(pos + w // 2, seq) - jnp.maximum(pos - w // 2, 0)).astype(F32)
                         for w in POOL_WINDOWS], axis=1)
    return pl.pallas_call(
        functools.partial(_pool_kernel, tuple(ends)),
        grid=(batch, seq // ts),
        in_specs=src_specs + [
            _const_spec((1, D_MODEL)),
            pl.BlockSpec((ts, len(POOL_WINDOWS)), lambda b, i: (i, 0)),
            _const_spec((len(POOL_WINDOWS), POOL_GROUP, POOL_GROUP)),
            _const_spec((1, D_MODEL)),
            _const_spec((1, D_MODEL)),
            _const_spec((D_MODEL, LANES)),
            _const_spec((1, LANES)),
            _const_spec((ts, ts)),
        ],
        out_specs=out_specs,
        out_shape=out_shapes,
        scratch_shapes=[pltpu.VMEM((ts + 2 * POOL_HALO, D_MODEL), F32)],
        compiler_params=pltpu.CompilerParams(
            dimension_semantics=("arbitrary", "arbitrary"), vmem_limit_bytes=VMEM_LIMIT),
        name="pool_mixer",
    )(*src_args, gmix, inv_cnt, pw, ps, gffn, wr, br, tri)


def _absorb_kernel(wk_ref, wqt_ref, out_ref):
    out_ref[...] = jnp.dot(wk_ref[...], wqt_ref[...], preferred_element_type=F32,
                           precision=lax.Precision.HIGHEST).astype(out_ref.dtype)


def _absorb_weights(wk, wqt):
    return pl.pallas_call(
        _absorb_kernel,
        grid=(N_HEADS,),
        in_specs=[pl.BlockSpec((None, KV_LORA, QK_NOPE), lambda h: (h, 0, 0)),
                  pl.BlockSpec((None, QK_NOPE, Q_LORA), lambda h: (h, 0, 0))],
        out_specs=pl.BlockSpec((None, KV_LORA, Q_LORA), lambda h: (h, 0, 0)),
        out_shape=jax.ShapeDtypeStruct((N_HEADS, KV_LORA, Q_LORA), BF16),
        name="absorb_weights",
    )(wk, wqt)


_NT = (((1,), (1,)), ((), ()))
_TN = (((0,), (0,)), ((), ()))


def _mla_proj_kernel(x_ref, gmix_ref, wdqt_ref, qn_ref, wkv_ref, wckvt_ref, kvn_row_ref, kvn_col_ref,
                     cos2_ref, sin2_ref, cqt_ref, ksh_ref, ckvt_ref):
    hb = (_rms(x_ref[...]) * gmix_ref[...]).astype(BF16)
    cqt = lax.dot_general(wdqt_ref[...], hb, _NT, preferred_element_type=F32)
    cqt_ref[...] = (_rms(cqt, axis=0) * qn_ref[...]).astype(BF16)
    kv = jnp.dot(hb, wkv_ref[...], preferred_element_type=F32)
    ckv = _rms(kv[:, 0:KV_LORA]) * kvn_row_ref[...]
    kr = kv[:, LANES:2 * LANES] * cos2_ref[...] + kv[:, 2 * LANES:3 * LANES] * sin2_ref[...]
    ksh_ref[:, 0:LANES] = ckv.astype(BF16)
    ksh_ref[:, LANES:2 * LANES] = kr.astype(BF16)
    ckvt = lax.dot_general(wckvt_ref[...], hb, _NT, preferred_element_type=F32)
    ckvt_ref[0:KV_LORA, :] = (_rms(ckvt, axis=0) * kvn_col_ref[...]).astype(BF16)
    ckvt_ref[KV_LORA:VT_ROWS, :] = jnp.ones((VT_ROWS - KV_LORA, ckvt.shape[1]), BF16)


def _mla_proj(x, gmix, wdqt, qn_col, wkv, wckvt, kvn_row, kvn_col, cos2, sin2):
    batch, seq, _ = x.shape
    ts = min(SEQ_TILE, seq)
    return pl.pallas_call(
        _mla_proj_kernel,
        grid=(batch, seq // ts),
        in_specs=[
            pl.BlockSpec((None, ts, D_MODEL), lambda b, i: (b, i, 0)),
            _const_spec((1, D_MODEL)),
            _const_spec((Q_LORA, D_MODEL)),
            _const_spec((Q_LORA, 1)),
            _const_spec((D_MODEL, 3 * LANES)),
            _const_spec((KV_LORA, D_MODEL)),
            _const_spec((1, KV_LORA)),
            _const_spec((KV_LORA, 1)),
            pl.BlockSpec((ts, LANES), lambda b, i: (i, 0)),
            pl.BlockSpec((ts, LANES), lambda b, i: (i, 0)),
        ],
        out_specs=(pl.BlockSpec((None, Q_LORA, ts), lambda b, i: (b, 0, i)),
                   pl.BlockSpec((None, ts, KEY_WIDTH), lambda b, i: (b, i, 0)),
                   pl.BlockSpec((None, VT_ROWS, ts), lambda b, i: (b, 0, i))),
        out_shape=(jax.ShapeDtypeStruct((batch, Q_LORA, seq), BF16),
                   jax.ShapeDtypeStruct((batch, seq, KEY_WIDTH), BF16),
                   jax.ShapeDtypeStruct((batch, VT_ROWS, seq), BF16)),
        compiler_params=pltpu.CompilerParams(
            dimension_semantics=("arbitrary", "arbitrary"), vmem_limit_bytes=VMEM_LIMIT),
        name="mla_proj",
    )(x, gmix, wdqt, qn_col, wkv, wckvt, kvn_row, kvn_col, cos2, sin2)


def _attn_kernel(x_ref, cqt_ref, ksh_ref, ckvt_ref, cost_ref, sint_ref, wabst_ref, wqrt_ref, wvt_ref, wo_ref,
                 gffn_ref, wr_ref, br_ref, tri_ref, x1_ref, h2_ref, slab_ref, cnt_ref,
                 o_scr, s0_scr, s1_scr, q0_scr, q1_scr):
    tq = x_ref.shape[0]
    half = QK_ROPE // 2

    @pl.when(_first_step())
    def _():
        cnt_ref[...] = jnp.zeros_like(cnt_ref)

    cq = cqt_ref[...]
    cost = cost_ref[...]
    sint = sint_ref[...]
    pad = jnp.zeros((KEY_WIDTH - KV_LORA - QK_ROPE, tq), F32)

    n_chunks = ksh_ref.shape[0] // KEY_CHUNK

    def make_qp(h):
        qn = jnp.dot(wabst_ref[h], cq, preferred_element_type=F32)
        qr = jnp.dot(wqrt_ref[h], cq, preferred_element_type=F32)
        a, b = qr[0:half], qr[half:QK_ROPE]
        qp = jnp.concatenate([qn, a * cost - b * sint, b * cost + a * sint, pad], axis=0)
        return (qp * SCORE_SCALE).astype(BF16)

    def stage(h_out, mx_out, s_out, h_in, s_in, q_in, h_next, q_next):
        if h_next is not None:
            q_next[...] = make_qp(h_next)
        if h_in is not None:
            qp = q_in[...]
            mx_in = jnp.full((8, tq), -jnp.inf, F32)
        else:
            mx_in = None
        if h_out is not None:
            m = jnp.max(mx_out, axis=0, keepdims=True)
            acc = jnp.zeros((VT_ROWS, tq), F32)
        for c in range(n_chunks):
            k0, k1 = c * KEY_CHUNK, (c + 1) * KEY_CHUNK
            if h_in is not None:
                s = jnp.dot(ksh_ref[k0:k1, :], qp, preferred_element_type=F32)
                s_in[k0:k1, :] = s
                mx_in = jnp.maximum(mx_in, jnp.max(s.reshape(KEY_CHUNK // 8, 8, tq), axis=0))
            if h_out is not None and (c + 1) % PV_CHUNKS == 0:
                k0 = k1 - PV_CHUNKS * KEY_CHUNK
                p = jnp.exp2(s_out[k0:k1, :] - m).astype(BF16)
                acc = acc + jnp.dot(ckvt_ref[:, k0:k1], p, preferred_element_type=F32)
        if h_out is not None:
            on = acc[0:KV_LORA] * (1.0 / acc[KV_LORA:KV_LORA + 1])
            ov = jnp.dot(wvt_ref[h_out], on.astype(BF16), preferred_element_type=F32)
            o_scr[pl.ds(pl.multiple_of(h_out * V_HEAD, V_HEAD), V_HEAD), :] = ov.astype(BF16)
        return mx_in

    q0_scr[...] = make_qp(0)
    mx = stage(None, None, None, 0, s0_scr, q0_scr, 1, q1_scr)

    def head_pair(j, mx):
        mx = stage(2 * j, mx, s0_scr, 2 * j + 1, s1_scr, q1_scr, 2 * j + 2, q0_scr)
        return stage(2 * j + 1, mx, s1_scr, 2 * j + 2, s0_scr, q0_scr, 2 * j + 3, q1_scr)

    mx = lax.fori_loop(0, N_HEADS // 2 - 1, head_pair, mx)
    mx = stage(N_HEADS - 2, mx, s0_scr, N_HEADS - 1, s1_scr, q1_scr, None, None)
    stage(N_HEADS - 1, mx, s1_scr, None, None, None, None, None)
    attn = lax.dot_general(o_scr[...], wo_ref[...], _TN, preferred_element_type=F32)
    x1 = x_ref[...] + attn
    x1_ref[...] = x1
    _ffn_norm_and_route(x1, gffn_ref, wr_ref, br_ref, tri_ref, h2_ref, slab_ref, cnt_ref)


def _attn_layer(x, cqt, ksh, ckvt, cost, sint, wabst, wqrt, wvt, wo, gffn, wr, br, tri):
    batch, seq, _ = x.shape
    tq = min(SEQ_TILE, seq)
    out_shapes, out_specs = _tail_out(batch, seq, tq)
    return pl.pallas_call(
        _attn_kernel,
        grid=(batch, seq // tq),
        in_specs=[
            pl.BlockSpec((None, tq, D_MODEL), lambda b, i: (b, i, 0)),
            pl.BlockSpec((None, Q_LORA, tq), lambda b, i: (b, 0, i)),
            pl.BlockSpec((None, seq, KEY_WIDTH), lambda b, i: (b, 0, 0)),
            pl.BlockSpec((None, VT_ROWS, seq), lambda b, i: (b, 0, 0)),
            pl.BlockSpec((QK_ROPE // 2, tq), lambda b, i: (0, i)),
            pl.BlockSpec((QK_ROPE // 2, tq), lambda b, i: (0, i)),
            _const_spec((N_HEADS, KV_LORA, Q_LORA)),
            _const_spec((N_HEADS, QK_ROPE, Q_LORA)),
            _const_spec((N_HEADS, V_HEAD, KV_LORA)),
            _const_spec((N_HEADS * V_HEAD, D_MODEL)),
            _const_spec((1, D_MODEL)),
            _const_spec((D_MODEL, LANES)),
            _const_spec((1, LANES)),
            _const_spec((tq, tq)),
        ],
        out_specs=out_specs,
        out_shape=out_shapes,
        scratch_shapes=[pltpu.VMEM((N_HEADS * V_HEAD, tq), BF16),
                        pltpu.VMEM((seq, tq), F32), pltpu.VMEM((seq, tq), F32),
                        pltpu.VMEM((KEY_WIDTH, tq), BF16), pltpu.VMEM((KEY_WIDTH, tq), BF16)],
        compiler_params=pltpu.CompilerParams(
            dimension_semantics=("arbitrary", "arbitrary"), vmem_limit_bytes=VMEM_LIMIT),
        name="mla_attention",
    )(x, cqt, ksh, ckvt, cost, sint, wabst, wqrt, wvt, wo, gffn, wr, br, tri)


def _token_copy(src, src_tok, dst, dst_tok, sem):
    s = pl.multiple_of(src_tok * PACK_ROWS, PACK_ROWS)
    d = pl.multiple_of(dst_tok * PACK_ROWS, PACK_ROWS)
    return pltpu.make_async_copy(src.at[pl.ds(s, PACK_ROWS), :], dst.at[pl.ds(d, PACK_ROWS), :], sem)


def _for_each_token(ts, fn):
    def body(it, c):
        for u in range(ROW_UNROLL):
            fn(it * ROW_UNROLL + u)
        return c
    lax.fori_loop(0, ts // ROW_UNROLL, body, 0)


def _scatter_kernel(zb_ref, zv_ref, h2_hbm, dest_hbm, xs_hbm, idx_smem, stage, zeros_vmem,
                    sem_idx, sem_stage, sem_zero, sem_rows):
    i = pl.program_id(0)
    n_tiles = pl.num_programs(0)
    ts = idx_smem.shape[0] // (2 * IDX_SLOTS)
    tile_rows = ts * PACK_ROWS
    zrows = zeros_vmem.shape[0]
    n_zero = zb_ref.shape[0]

    def zero_copy(e):
        start = pl.multiple_of(zb_ref[e] * PACK_ROWS, zrows)
        return pltpu.make_async_copy(zeros_vmem, xs_hbm.at[pl.ds(start, zrows), :], sem_zero)

    @pl.when(i == 0)
    def _():
        zeros_vmem[...] = jnp.zeros_like(zeros_vmem)

        def start(e, c):
            @pl.when(zv_ref[e] > 0)
            def _():
                zero_copy(e).start()
            return c

        def wait(e, c):
            @pl.when(zv_ref[e] > 0)
            def _():
                zero_copy(e).wait()
            return c

        lax.fori_loop(0, n_zero, start, 0)
        lax.fori_loop(0, n_zero, wait, 0)

    def fetches(t):
        slot = lax.rem(t, IDX_SLOTS)
        rows = h2_hbm.at[pl.ds(pl.multiple_of(t * tile_rows, tile_rows), tile_rows), :]
        return [pltpu.make_async_copy(dest_hbm.at[t], idx_smem.at[pl.ds(slot * (2 * ts), 2 * ts)], sem_idx.at[slot]),
                pltpu.make_async_copy(rows, stage.at[slot], sem_stage.at[slot])]

    def copies(t, r):
        slot = lax.rem(t, IDX_SLOTS)
        sem = sem_rows.at[lax.rem(t, 2)]
        return [_token_copy(stage.at[slot], r, xs_hbm, idx_smem[slot * (2 * ts) + 2 * r + k], sem)
                for k in range(2)]

    def start_tile(t):
        _for_each_token(ts, lambda r: [c.start(priority=k) for k, c in enumerate(copies(t, r))])

    def wait_tile(t):
        _for_each_token(ts, lambda r: [c.wait() for c in copies(t, r)])

    @pl.when(i == 0)
    def _():
        for c in fetches(i):
            c.start()

    for c in fetches(i):
        c.wait()

    @pl.when(i + 1 < n_tiles)
    def _():
        for c in fetches(i + 1):
            c.start()

    start_tile(i)

    @pl.when(i > 0)
    def _():
        wait_tile(i - 1)

    @pl.when(i == n_tiles - 1)
    def _():
        wait_tile(i)


def _scatter_rows(h2, dest, zb, zv, n_rows, bm):
    ts = dest.shape[1] // 2
    tokens = h2.shape[0] // PACK_ROWS
    return pl.pallas_call(
        _scatter_kernel,
        grid_spec=pltpu.PrefetchScalarGridSpec(
            num_scalar_prefetch=2,
            grid=(tokens // ts,),
            in_specs=[pl.BlockSpec(memory_space=pl.ANY),
                      pl.BlockSpec(memory_space=pl.ANY)],
            out_specs=pl.BlockSpec(memory_space=pl.ANY),
            scratch_shapes=[pltpu.SMEM((IDX_SLOTS * 2 * ts,), jnp.int32),
                            pltpu.VMEM((IDX_SLOTS, ts * PACK_ROWS, LANES), jnp.uint32),
                            pltpu.VMEM((bm * PACK_ROWS, LANES), jnp.uint32),
                            pltpu.SemaphoreType.DMA((IDX_SLOTS,)), pltpu.SemaphoreType.DMA((IDX_SLOTS,)),
                            pltpu.SemaphoreType.DMA, pltpu.SemaphoreType.DMA((2,))],
        ),
        out_shape=jax.ShapeDtypeStruct((n_rows * PACK_ROWS, LANES), jnp.uint32),
        compiler_params=pltpu.CompilerParams(
            dimension_semantics=("arbitrary",), vmem_limit_bytes=VMEM_LIMIT),
        name="moe_scatter",
    )(zb, zv, h2, dest)


def _expert_kernel(be_ref, nv_ref, xs_ref, wg_ref, wu_ref, wd_ref, ys_ref, wg_bf, wu_bf, wd_bf):
    b = pl.program_id(0)

    @pl.when(b < nv_ref[0])
    def _():
        changed = (b == 0) | (be_ref[b] != be_ref[jnp.maximum(b - 1, 0)])

        @pl.when(changed)
        def _():
            wg_bf[...] = wg_ref[...].astype(BF16)
            wu_bf[...] = wu_ref[...].astype(BF16)
            wd_bf[...] = wd_ref[...].astype(BF16)

        xb = _from_packed_tiles(xs_ref, xs_ref.shape[0] // PACK_ROWS)
        gate = jnp.dot(xb, wg_bf[...], preferred_element_type=F32)
        up = jnp.dot(xb, wu_bf[...], preferred_element_type=F32)
        hid = gate * (1.0 / (1.0 + jnp.exp(-gate))) * up
        _to_packed_tiles(ys_ref, jnp.dot(hid.astype(BF16), wd_bf[...], preferred_element_type=F32))

    @pl.when(b >= nv_ref[0])
    def _():
        ys_ref[...] = jnp.zeros_like(ys_ref)


def _expert_mlp(xs, block_expert, n_valid, wg, wu, wd, layer, bm):
    n_blocks = xs.shape[0] // (bm * PACK_ROWS)
    blk = lambda b, be, nv: (jnp.minimum(b, nv[0] - 1), 0)
    out_blk = lambda b, be, nv: (b, 0)
    wsel = lambda b, be, nv: (layer, be[jnp.minimum(b, nv[0] - 1)], 0, 0)
    return pl.pallas_call(
        _expert_kernel,
        grid_spec=pltpu.PrefetchScalarGridSpec(
            num_scalar_prefetch=2,
            grid=(n_blocks,),
            in_specs=[pl.BlockSpec((bm * PACK_ROWS, LANES), blk),
                      pl.BlockSpec((None, None, D_MODEL, D_EXPERT), wsel),
                      pl.BlockSpec((None, None, D_MODEL, D_EXPERT), wsel),
                      pl.BlockSpec((None, None, D_EXPERT, D_MODEL), wsel)],
            out_specs=pl.BlockSpec((bm * PACK_ROWS, LANES), out_blk),
            scratch_shapes=[pltpu.VMEM((D_MODEL, D_EXPERT), BF16),
                            pltpu.VMEM((D_MODEL, D_EXPERT), BF16),
                            pltpu.VMEM((D_EXPERT, D_MODEL), BF16)],
        ),
        out_shape=jax.ShapeDtypeStruct(xs.shape, jnp.uint32),
        compiler_params=pltpu.CompilerParams(
            dimension_semantics=("arbitrary",), vmem_limit_bytes=VMEM_LIMIT),
        name="moe_experts",
    )(block_expert, n_valid, xs, wg, wu, wd)


def _combine_kernel(first_tiles, x1_ref, slab_ref, dest_hbm, ys_hbm, *rest):
    if first_tiles is None:
        x2_ref, idx_smem, g0, g1, sem_idx, sem_rows = rest
    else:
        gfin_ref, out_a_ref, out_b_ref, idx_smem, g0, g1, sem_idx, sem_rows = rest
    i = pl.program_id(0)
    n_tiles = pl.num_programs(0)
    ts = x1_ref.shape[0]

    def idx_copy(t):
        slot = lax.rem(t, IDX_SLOTS)
        return pltpu.make_async_copy(dest_hbm.at[t], idx_smem.at[pl.ds(slot * (2 * ts), 2 * ts)], sem_idx.at[slot])

    def copies(t, r):
        slot = lax.rem(t, IDX_SLOTS)
        half = lax.rem(t, 2)
        return [_token_copy(ys_hbm, idx_smem[slot * (2 * ts) + 2 * r + k], g.at[half], r, sem_rows.at[half])
                for k, g in enumerate((g0, g1))]

    def start_tile(t):
        _for_each_token(ts, lambda r: [c.start(priority=k) for k, c in enumerate(copies(t, r))])

    @pl.when(i == 0)
    def _():
        idx_copy(i).start()
        idx_copy(i).wait()
        start_tile(i)

        @pl.when(n_tiles > 1)
        def _():
            idx_copy(i + 1).start()

    @pl.when(i + 1 < n_tiles)
    def _():
        idx_copy(i + 1).wait()
        start_tile(i + 1)

    @pl.when(i + 2 < n_tiles)
    def _():
        idx_copy(i + 2).start()

    _for_each_token(ts, lambda r: [c.wait() for c in copies(i, r)])
    half = lax.rem(i, 2)
    slab = slab_ref[...]
    y = (slab[:, L_W1:L_W1 + 1] * _from_packed_tiles(g0.at[half], ts).astype(F32)
         + slab[:, L_W2:L_W2 + 1] * _from_packed_tiles(g1.at[half], ts).astype(F32))
    x2 = x1_ref[...] + y
    if first_tiles is None:
        x2_ref[...] = x2
    else:
        out = _rms(x2) * gfin_ref[...]

        @pl.when(i < first_tiles)
        def _():
            out_a_ref[...] = out

        @pl.when(i >= first_tiles)
        def _():
            out_b_ref[...] = out


def _combine_rows(x1, slab, dest, ys, final=None):
    tokens = x1.shape[0]
    ts = dest.shape[1] // 2
    tok = lambda i: (i, 0)
    in_specs = [pl.BlockSpec((ts, D_MODEL), tok),
                pl.BlockSpec((ts, LANES), tok),
                pl.BlockSpec(memory_space=pl.ANY),
                pl.BlockSpec(memory_space=pl.ANY)]
    args = [x1, slab, dest, ys]
    if final is None:
        first_tiles = None
        out_specs = pl.BlockSpec((ts, D_MODEL), tok)
        out_shape = jax.ShapeDtypeStruct(x1.shape, F32)
    else:
        gain, first_tokens = final
        first_tiles = first_tokens // ts
        in_specs.append(pl.BlockSpec((1, D_MODEL), lambda i: (0, 0)))
        args.append(gain)
        out_specs = (pl.BlockSpec((ts, D_MODEL), lambda i: (jnp.minimum(i, first_tiles - 1), 0)),
                     pl.BlockSpec((ts, D_MODEL), lambda i: (jnp.maximum(i - first_tiles, 0), 0)))
        out_shape = (jax.ShapeDtypeStruct((first_tokens, D_MODEL), F32),
                     jax.ShapeDtypeStruct((tokens - first_tokens, D_MODEL), F32))
    return pl.pallas_call(
        functools.partial(_combine_kernel, first_tiles),
        grid=(tokens // ts,),
        in_specs=in_specs,
        out_specs=out_specs,
        out_shape=out_shape,
        scratch_shapes=[pltpu.SMEM((IDX_SLOTS * 2 * ts,), jnp.int32),
                        pltpu.VMEM((2, ts * PACK_ROWS, LANES), jnp.uint32),
                        pltpu.VMEM((2, ts * PACK_ROWS, LANES), jnp.uint32),
                        pltpu.SemaphoreType.DMA((IDX_SLOTS,)), pltpu.SemaphoreType.DMA((2,))],
        compiler_params=pltpu.CompilerParams(
            dimension_semantics=("arbitrary",), vmem_limit_bytes=VMEM_LIMIT),
        name="moe_combine",
    )(*args)


def _moe(x1, h2, slab, counts, wg, wu, wd, layer, final=None):
    tokens = x1.shape[0]
    bm = MOE_BLOCK
    ts = min(ROW_TILE, tokens)
    n_blocks = -(-2 * tokens // bm) + N_EXPERTS
    cnt = counts[0, :N_EXPERTS].astype(jnp.int32)
    padded = ((cnt + bm - 1) // bm) * bm
    pend = jnp.cumsum(padded)
    pstart = pend - padded
    eid = slab[:, L_E1:L_E2 + 1].astype(jnp.int32)
    rank = slab[:, L_R1:L_R2 + 1].astype(jnp.int32)
    hit = eid[:, :, None] == jnp.arange(N_EXPERTS, dtype=jnp.int32)
    dest = (jnp.sum(jnp.where(hit, pstart, 0), axis=-1) + rank).reshape(tokens // ts, 2 * ts)
    block_start = jnp.arange(n_blocks, dtype=jnp.int32) * bm
    block_expert = jnp.minimum(jnp.sum(pend[None, :] <= block_start[:, None], axis=1), N_EXPERTS - 1)
    n_valid = (pend[-1:] // bm).astype(jnp.int32)
    trailing = n_valid[0] + jnp.arange(N_EXPERTS, dtype=jnp.int32)
    zb = jnp.concatenate([pend - bm, trailing * bm]).astype(jnp.int32)
    zv = jnp.concatenate([padded > 0, trailing < n_blocks]).astype(jnp.int32)

    xs = _scatter_rows(h2, dest, zb, zv, n_blocks * bm, bm)
    ys = _expert_mlp(xs, block_expert.astype(jnp.int32), n_valid, wg, wu, wd, layer, bm)
    return _combine_rows(x1, slab, dest, ys, final)


def _rope_tables(seq):
    inv_freq = 1.0 / (ROPE_THETA ** (jnp.arange(0, QK_ROPE, 2, dtype=F32) / QK_ROPE))
    ang = jnp.arange(seq, dtype=F32)[:, None] * inv_freq[None, :]
    return jnp.cos(ang), jnp.sin(ang)


def _trunk(inputs, norm_mix, norm_ffn, norm_final, pool_w, pool_scale,
           mla_w_dq, mla_q_norm, mla_w_uq, mla_w_dkv, mla_kv_norm, mla_w_ukv, mla_w_o,
           moe_w_group, moe_b_group, moe_w_router, moe_b_router, moe_w_gate, moe_w_up, moe_w_down):
    seq = inputs[0].shape[1]
    batch = sum(a.shape[0] for a in inputs)
    x = None
    ts = min(SEQ_TILE, seq)
    tri = jnp.tril(jnp.ones((ts, ts), F32), k=-1).astype(BF16)
    cos, sin = _rope_tables(seq)
    lane_pad = ((0, 0), (0, LANES - QK_ROPE))
    cos2 = jnp.pad(jnp.concatenate([cos, cos], axis=1), lane_pad)
    sin2 = jnp.pad(jnp.concatenate([-sin, sin], axis=1), lane_pad)
    cost, sint = cos.T, sin.T
    row = lambda v: v.reshape(1, -1)
    col = lambda v: v.reshape(-1, 1)

    for i in range(DEPTH):
        j = i // 2
        pad = LANES - N_EXPERTS - N_GROUPS
        wr = jnp.pad(jnp.concatenate([moe_w_router[i], moe_w_group[i]], axis=1), ((0, 0), (0, pad))).astype(BF16)
        br = row(jnp.pad(jnp.concatenate([moe_b_router[i], moe_b_group[i]]), (0, pad)))
        gmix, gffn = row(norm_mix[i]), row(norm_ffn[i])
        if i % 2 == 0:
            x1, h2, slab, counts = _pool_layer(inputs if x is None else [x], gmix, pool_w[j].astype(BF16),
                                               row(pool_scale[j]), gffn, wr, br, tri)
        else:
            w_uq = mla_w_uq[j].reshape(Q_LORA, N_HEADS, QK_NOPE + QK_ROPE)
            w_ukv = mla_w_ukv[j].reshape(KV_LORA, N_HEADS, QK_NOPE + V_HEAD)
            wabst = _absorb_weights(w_ukv[:, :, :QK_NOPE].transpose(1, 0, 2),
                                    w_uq[:, :, :QK_NOPE].transpose(1, 2, 0))
            wqrt = w_uq[:, :, QK_NOPE:].transpose(1, 2, 0).astype(BF16)
            wvt = w_ukv[:, :, QK_NOPE:].transpose(1, 2, 0).astype(BF16)
            w_ckv = mla_w_dkv[j][:, :KV_LORA]
            w_kr = mla_w_dkv[j][:, KV_LORA:]
            half = QK_ROPE // 2
            w_kr_swapped = jnp.concatenate([w_kr[:, half:], w_kr[:, :half]], axis=1)
            wkv = jnp.concatenate([w_ckv, jnp.pad(w_kr, lane_pad), jnp.pad(w_kr_swapped, lane_pad)],
                                  axis=1).astype(BF16)
            cqt, ksh, ckvt = _mla_proj(x, gmix, mla_w_dq[j].T.astype(BF16), col(mla_q_norm[j]), wkv,
                                       w_ckv.T.astype(BF16), row(mla_kv_norm[j]), col(mla_kv_norm[j]),
                                       cos2, sin2)
            x1, h2, slab, counts = _attn_layer(x, cqt, ksh, ckvt, cost, sint, wabst, wqrt, wvt,
                                               mla_w_o[j].astype(BF16), gffn, wr, br, tri)
        if i < DEPTH - 1:
            x = _moe(x1, h2, slab, counts, moe_w_gate, moe_w_up, moe_w_down, i).reshape(batch, seq, D_MODEL)
    outs = _moe(x1, h2, slab, counts, moe_w_gate, moe_w_up, moe_w_down, DEPTH - 1,
                final=(row(norm_final), inputs[0].shape[0] * seq))
    return tuple(o.reshape(a.shape) for o, a in zip(outs, inputs))


def kernel(x_prompt, x_sample, norm_mix, norm_ffn, norm_final, pool_w, pool_scale, mla_w_dq, mla_q_norm,
           mla_w_uq, mla_w_dkv, mla_kv_norm, mla_w_ukv, mla_w_o, moe_w_group, moe_b_group, moe_w_router,
           moe_b_router, moe_w_gate, moe_w_up, moe_w_down):
    return _trunk([x_prompt, x_sample], norm_mix, norm_ffn, norm_final, pool_w, pool_scale,
                  mla_w_dq, mla_q_norm, mla_w_uq, mla_w_dkv, mla_kv_norm, mla_w_ukv, mla_w_o,
                  moe_w_group, moe_b_group, moe_w_router, moe_b_router, moe_w_gate, moe_w_up, moe_w_down)
```

```python
import functools
import math

import jax
import jax.numpy as jnp
from jax import lax
from jax.experimental import pallas as pl
from jax.experimental.pallas import tpu as pltpu

F32 = jnp.float32
BF16 = jnp.bfloat16

D_MODEL = 1024
DEPTH = 4
POOL_WINDOWS = (2, 4, 8, 16)
POOL_GROUP = D_MODEL // len(POOL_WINDOWS)
POOL_HALO = 8
N_HEADS = 16
QK_NOPE = 64
QK_ROPE = 32
V_HEAD = 64
Q_LORA = 256
KV_LORA = 128
ROPE_THETA = 10000.0
N_GROUPS = 8
EXPERTS_PER_GROUP = 8
N_EXPERTS = N_GROUPS * EXPERTS_PER_GROUP
D_EXPERT = 512
EPS = 1e-6

LANES = 128
TOK_ROWS = D_MODEL // LANES
PACK_ROWS = TOK_ROWS // 2
ROW_UNROLL = 128
IDX_SLOTS = 3
KEY_WIDTH = 2 * LANES
VT_ROWS = KV_LORA + 16
KEY_CHUNK = 256
PV_CHUNKS = 2
SEQ_TILE = 512
ROW_TILE = 512
MOE_BLOCK = 512
VMEM_LIMIT = 48 * 1024 * 1024
SCORE_SCALE = (QK_NOPE + QK_ROPE) ** -0.5 * math.log2(math.e)

L_E1, L_E2, L_W1, L_W2, L_R1, L_R2 = range(6)


def _rms(v, axis=-1):
    return v * lax.rsqrt(jnp.mean(v * v, axis=axis, keepdims=True) + EPS)


def _to_packed_tiles(ref, v):
    n = v.shape[0]
    bits = lambda a: lax.bitcast_convert_type(a.astype(BF16).astype(F32), jnp.uint32)
    for j in range(PACK_ROWS):
        hi = bits(v[:, j * LANES:(j + 1) * LANES])
        lo = bits(v[:, D_MODEL // 2 + j * LANES:D_MODEL // 2 + (j + 1) * LANES])
        ref[pl.ds(j, n, stride=PACK_ROWS), :] = hi | (lo >> 16)


def _from_packed_tiles(ref, n):
    his, los = [], []
    for j in range(PACK_ROWS):
        w = ref[pl.ds(j, n, stride=PACK_ROWS), :]
        his.append(lax.bitcast_convert_type(w & jnp.uint32(0xFFFF0000), F32).astype(BF16))
        los.append(lax.bitcast_convert_type(w << 16, F32).astype(BF16))
    return jnp.concatenate(his + los, axis=1)


def _ffn_norm_and_route(x1, gffn_ref, wr_ref, br_ref, tri_ref, h2_ref, slab_ref, cnt_ref):
    h2 = _rms(x1) * gffn_ref[...]
    _to_packed_tiles(h2_ref, h2)
    logits = jnp.dot(h2.astype(BF16), wr_ref[...], preferred_element_type=F32) + br_ref[...]
    lane = lax.broadcasted_iota(jnp.int32, logits.shape, 1)
    lanef = lane.astype(F32)
    neg = jnp.float32(-jnp.inf)
    big = jnp.float32(1e9)

    gmask = (lane >= N_EXPERTS) & (lane < N_EXPERTS + N_GROUPS)
    gl = jnp.where(gmask, logits, neg)
    gmax = jnp.max(gl, axis=-1, keepdims=True)
    gsum = jnp.sum(jnp.exp(gl - gmax), axis=-1, keepdims=True)
    g_p = 1.0 / gsum
    gidx = jnp.min(jnp.where(gl == gmax, lanef, big), axis=-1, keepdims=True) - N_EXPERTS
    lo = gidx * EXPERTS_PER_GROUP
    emask = (lanef >= lo) & (lanef < lo + EXPERTS_PER_GROUP)
    el = jnp.where(emask, logits, neg)
    m1 = jnp.max(el, axis=-1, keepdims=True)
    i1 = jnp.min(jnp.where(el == m1, lanef, big), axis=-1, keepdims=True)
    el2 = jnp.where(lanef == i1, neg, el)
    m2 = jnp.max(el2, axis=-1, keepdims=True)
    i2 = jnp.min(jnp.where(el2 == m2, lanef, big), axis=-1, keepdims=True)
    t = jnp.exp(m2 - m1)
    w1 = g_p / (1.0 + t)
    w2 = w1 * t

    oh1 = jnp.where(lanef == i1, 1.0, 0.0)
    oh2 = jnp.where(lanef == i2, 1.0, 0.0)
    oh = oh1 + oh2
    before = jnp.dot(tri_ref[...], oh.astype(BF16), preferred_element_type=F32) + cnt_ref[...]
    r1 = jnp.sum(oh1 * before, axis=-1, keepdims=True)
    r2 = jnp.sum(oh2 * before, axis=-1, keepdims=True)
    cnt_ref[...] = cnt_ref[...] + jnp.sum(oh, axis=0, keepdims=True)

    slab = jnp.zeros_like(logits)
    for l, v in ((L_E1, i1), (L_E2, i2), (L_W1, w1), (L_W2, w2), (L_R1, r1), (L_R2, r2)):
        slab = jnp.where(lane == l, v, slab)
    slab_ref[...] = slab


def _first_step():
    return (pl.program_id(0) == 0) & (pl.program_id(1) == 0)


def _pool_kernel(batch_ends, *refs):
    n_src = len(batch_ends)
    src_refs, refs = refs[:3 * n_src], refs[3 * n_src:]
    (gmix_ref, inv_ref, pw_ref, ps_ref, gffn_ref, wr_ref, br_ref, tri_ref,
     x1_ref, h2_ref, slab_ref, cnt_ref, hcat_ref) = refs
    b = pl.program_id(0)
    i = pl.program_id(1)
    ts = x1_ref.shape[0]

    def pick(which):
        v = src_refs[3 * (n_src - 1) + which][...]
        for k in range(n_src - 2, -1, -1):
            v = jnp.where(b < batch_ends[k], src_refs[3 * k + which][...], v)
        return v

    @pl.when(_first_step())
    def _():
        cnt_ref[...] = jnp.zeros_like(cnt_ref)

    g = gmix_ref[...]
    x = pick(0)
    h = _rms(x) * g
    hp = jnp.where(i > 0, _rms(pick(1)) * g, 0.0)
    hn = jnp.where(i < pl.num_programs(1) - 1, _rms(pick(2)) * g, 0.0)
    hcat_ref[0:POOL_HALO, :] = hp
    hcat_ref[POOL_HALO:POOL_HALO + ts, :] = h
    hcat_ref[POOL_HALO + ts:2 * POOL_HALO + ts, :] = hn

    inv_cnt = inv_ref[...]
    for gi, w in enumerate(POOL_WINDOWS):
        c0 = gi * POOL_GROUP
        acc = None
        for j in range(-(w // 2), w // 2):
            v = hcat_ref[POOL_HALO + j:POOL_HALO + j + ts, c0:c0 + POOL_GROUP]
            acc = v if acc is None else acc + v
        pooled = acc * inv_cnt[:, gi:gi + 1] - h[:, c0:c0 + POOL_GROUP]
        mixed = jnp.dot(pooled.astype(BF16), pw_ref[gi], preferred_element_type=F32)
        x1_ref[:, c0:c0 + POOL_GROUP] = x[:, c0:c0 + POOL_GROUP] + mixed * ps_ref[:, c0:c0 + POOL_GROUP]

    _ffn_norm_and_route(x1_ref[...], gffn_ref, wr_ref, br_ref, tri_ref, h2_ref, slab_ref, cnt_ref)


def _tail_out(batch, seq, ts):
    tok = lambda b, i: (b * (seq // ts) + i, 0)
    shapes = (jax.ShapeDtypeStruct((batch * seq, D_MODEL), F32),
              jax.ShapeDtypeStruct((batch * seq * PACK_ROWS, LANES), jnp.uint32),
              jax.ShapeDtypeStruct((batch * seq, LANES), F32),
              jax.ShapeDtypeStruct((1, LANES), F32))
    specs = (pl.BlockSpec((ts, D_MODEL), tok), pl.BlockSpec((ts * PACK_ROWS, LANES), tok),
             pl.BlockSpec((ts, LANES), tok), pl.BlockSpec((1, LANES), lambda b, i: (0, 0)))
    return shapes, specs


def _const_spec(shape):
    nd = len(shape)
    return pl.BlockSpec(shape, lambda b, i: (0,) * nd)


def _source_specs(first, count, ts, seq):
    hb = ts // POOL_HALO
    nh = seq // POOL_HALO

    def spec(rows, row_block):
        def index(b, i):
            mine = (b >= first) & (b < first + count)
            return (jnp.clip(b - first, 0, count - 1), jnp.where(mine, row_block(i), 0), 0)
        return pl.BlockSpec((None, rows, D_MODEL), index)

    return [spec(ts, lambda i: i),
            spec(POOL_HALO, lambda i: jnp.maximum(i * hb - 1, 0)),
            spec(POOL_HALO, lambda i: jnp.minimum((i + 1) * hb, nh - 1))]


def _pool_layer(sources, gmix, pw, ps, gffn, wr, br, tri):
    seq = sources[0].shape[1]
    ts = min(SEQ_TILE, seq)
    src_specs, src_args, ends, first = [], [], [], 0
    for src in sources:
        src_specs += _source_specs(first, src.shape[0], ts, seq)
        src_args += [src, src, src]
        first += src.shape[0]
        ends.append(first)
    batch = first
    out_shapes, out_specs = _tail_out(batch, seq, ts)
    pos = jnp.arange(seq)
    inv_cnt = jnp.stack([1.0 / (jnp.minimum(pos + w // 2, seq) - jnp.maximum(pos - w // 2, 0)).astype(F32)
                         for w in POOL_WINDOWS], axis=1)
    return pl.pallas_call(
        functools.partial(_pool_kernel, tuple(ends)),
        grid=(batch, seq // ts),
        in_specs=src_specs + [
            _const_spec((1, D_MODEL)),
            pl.BlockSpec((ts, len(POOL_WINDOWS)), lambda b, i: (i, 0)),
            _const_spec((len(POOL_WINDOWS), POOL_GROUP, POOL_GROUP)),
            _const_spec((1, D_MODEL)),
            _const_spec((1, D_MODEL)),
            _const_spec((D_MODEL, LANES)),
            _const_spec((1, LANES)),
            _const_spec((ts, ts)),
        ],
        out_specs=out_specs,
        out_shape=out_shapes,
        scratch_shapes=[pltpu.VMEM((ts + 2 * POOL_HALO, D_MODEL), F32)],
        compiler_params=pltpu.CompilerParams(
            dimension_semantics=("arbitrary", "arbitrary"), vmem_limit_bytes=VMEM_LIMIT),
        name="pool_mixer",
    )(*src_args, gmix, inv_cnt, pw, ps, gffn, wr, br, tri)


def _absorb_kernel(wk_ref, wqt_ref, out_ref):
    out_ref[...] = jnp.dot(wk_ref[...], wqt_ref[...], preferred_element_type=F32,
                           precision=lax.Precision.HIGHEST).astype(out_ref.dtype)


def _absorb_weights(wk, wqt):
    return pl.pallas_call(
        _absorb_kernel,
        grid=(N_HEADS,),
        in_specs=[pl.BlockSpec((None, KV_LORA, QK_NOPE), lambda h: (h, 0, 0)),
                  pl.BlockSpec((None, QK_NOPE, Q_LORA), lambda h: (h, 0, 0))],
        out_specs=pl.BlockSpec((None, KV_LORA, Q_LORA), lambda h: (h, 0, 0)),
        out_shape=jax.ShapeDtypeStruct((N_HEADS, KV_LORA, Q_LORA), BF16),
        name="absorb_weights",
    )(wk, wqt)


_NT = (((1,), (1,)), ((), ()))
_TN = (((0,), (0,)), ((), ()))


def _mla_proj_kernel(x_ref, gmix_ref, wdqt_ref, qn_ref, wkv_ref, wckvt_ref, kvn_row_ref, kvn_col_ref,
                     cos2_ref, sin2_ref, cqt_ref, ksh_ref, ckvt_ref):
    hb = (_rms(x_ref[...]) * gmix_ref[...]).astype(BF16)
    cqt = lax.dot_general(wdqt_ref[...], hb, _NT, preferred_element_type=F32)
    cqt_ref[...] = (_rms(cqt, axis=0) * qn_ref[...]).astype(BF16)
    kv = jnp.dot(hb, wkv_ref[...], preferred_element_type=F32)
    ckv = _rms(kv[:, 0:KV_LORA]) * kvn_row_ref[...]
    kr = kv[:, LANES:2 * LANES] * cos2_ref[...] + kv[:, 2 * LANES:3 * LANES] * sin2_ref[...]
    ksh_ref[:, 0:LANES] = ckv.astype(BF16)
    ksh_ref[:, LANES:2 * LANES] = kr.astype(BF16)
    ckvt = lax.dot_general(wckvt_ref[...], hb, _NT, preferred_element_type=F32)
    ckvt_ref[0:KV_LORA, :] = (_rms(ckvt, axis=0) * kvn_col_ref[...]).astype(BF16)
    ckvt_ref[KV_LORA:VT_ROWS, :] = jnp.ones((VT_ROWS - KV_LORA, ckvt.shape[1]), BF16)


def _mla_proj(x, gmix, wdqt, qn_col, wkv, wckvt, kvn_row, kvn_col, cos2, sin2):
    batch, seq, _ = x.shape
    ts = min(SEQ_TILE, seq)
    return pl.pallas_call(
        _mla_proj_kernel,
        grid=(batch, seq // ts),
        in_specs=[
            pl.BlockSpec((None, ts, D_MODEL), lambda b, i: (b, i, 0)),
            _const_spec((1, D_MODEL)),
            _const_spec((Q_LORA, D_MODEL)),
            _const_spec((Q_LORA, 1)),
            _const_spec((D_MODEL, 3 * LANES)),
            _const_spec((KV_LORA, D_MODEL)),
            _const_spec((1, KV_LORA)),
            _const_spec((KV_LORA, 1)),
            pl.BlockSpec((ts, LANES), lambda b, i: (i, 0)),
            pl.BlockSpec((ts, LANES), lambda b, i: (i, 0)),
        ],
        out_specs=(pl.BlockSpec((None, Q_LORA, ts), lambda b, i: (b, 0, i)),
                   pl.BlockSpec((None, ts, KEY_WIDTH), lambda b, i: (b, i, 0)),
                   pl.BlockSpec((None, VT_ROWS, ts), lambda b, i: (b, 0, i))),
        out_shape=(jax.ShapeDtypeStruct((batch, Q_LORA, seq), BF16),
                   jax.ShapeDtypeStruct((batch, seq, KEY_WIDTH), BF16),
                   jax.ShapeDtypeStruct((batch, VT_ROWS, seq), BF16)),
        compiler_params=pltpu.CompilerParams(
            dimension_semantics=("arbitrary", "arbitrary"), vmem_limit_bytes=VMEM_LIMIT),
        name="mla_proj",
    )(x, gmix, wdqt, qn_col, wkv, wckvt, kvn_row, kvn_col, cos2, sin2)


def _attn_kernel(x_ref, cqt_ref, ksh_ref, ckvt_ref, cost_ref, sint_ref, wabst_ref, wqrt_ref, wvt_ref, wo_ref,
                 gffn_ref, wr_ref, br_ref, tri_ref, x1_ref, h2_ref, slab_ref, cnt_ref,
                 o_scr, s0_scr, s1_scr, q0_scr, q1_scr):
    tq = x_ref.shape[0]
    half = QK_ROPE // 2

    @pl.when(_first_step())
    def _():
        cnt_ref[...] = jnp.zeros_like(cnt_ref)

    cq = cqt_ref[...]
    cost = cost_ref[...]
    sint = sint_ref[...]
    pad = jnp.zeros((KEY_WIDTH - KV_LORA - QK_ROPE, tq), F32)

    n_chunks = ksh_ref.shape[0] // KEY_CHUNK

    def make_qp(h):
        qn = jnp.dot(wabst_ref[h], cq, preferred_element_type=F32)
        qr = jnp.dot(wqrt_ref[h], cq, preferred_element_type=F32)
        a, b = qr[0:half], qr[half:QK_ROPE]
        qp = jnp.concatenate([qn, a * cost - b * sint, b * cost + a * sint, pad], axis=0)
        return (qp * SCORE_SCALE).astype(BF16)

    def stage(h_out, mx_out, s_out, h_in, s_in, q_in, h_next, q_next):
        if h_next is not None:
            q_next[...] = make_qp(h_next)
        if h_in is not None:
            qp = q_in[...]
            mx_in = jnp.full((8, tq), -jnp.inf, F32)
        else:
            mx_in = None
        if h_out is not None:
            m = jnp.max(mx_out, axis=0, keepdims=True)
            acc = jnp.zeros((VT_ROWS, tq), F32)
        for c in range(n_chunks):
            k0, k1 = c * KEY_CHUNK, (c + 1) * KEY_CHUNK
            if h_in is not None:
                s = jnp.dot(ksh_ref[k0:k1, :], qp, preferred_element_type=F32)
                s_in[k0:k1, :] = s
                mx_in = jnp.maximum(mx_in, jnp.max(s.reshape(KEY_CHUNK // 8, 8, tq), axis=0))
            if h_out is not None and (c + 1) % PV_CHUNKS == 0:
                k0 = k1 - PV_CHUNKS * KEY_CHUNK
                p = jnp.exp2(s_out[k0:k1, :] - m).astype(BF16)
                acc = acc + jnp.dot(ckvt_ref[:, k0:k1], p, preferred_element_type=F32)
        if h_out is not None:
            on = acc[0:KV_LORA] * (1.0 / acc[KV_LORA:KV_LORA + 1])
            ov = jnp.dot(wvt_ref[h_out], on.astype(BF16), preferred_element_type=F32)
            o_scr[pl.ds(pl.multiple_of(h_out * V_HEAD, V_HEAD), V_HEAD), :] = ov.astype(BF16)
        return mx_in

    q0_scr[...] = make_qp(0)
    mx = stage(None, None, None, 0, s0_scr, q0_scr, 1, q1_scr)

    def head_pair(j, mx):
        mx = stage(2 * j, mx, s0_scr, 2 * j + 1, s1_scr, q1_scr, 2 * j + 2, q0_scr)
        return stage(2 * j + 1, mx, s1_scr, 2 * j + 2, s0_scr, q0_scr, 2 * j + 3, q1_scr)

    mx = lax.fori_loop(0, N_HEADS // 2 - 1, head_pair, mx)
    mx = stage(N_HEADS - 2, mx, s0_scr, N_HEADS - 1, s1_scr, q1_scr, None, None)
    stage(N_HEADS - 1, mx, s1_scr, None, None, None, None, None)
    attn = lax.dot_general(o_scr[...], wo_ref[...], _TN, preferred_element_type=F32)
    x1 = x_ref[...] + attn
    x1_ref[...] = x1
    _ffn_norm_and_route(x1, gffn_ref, wr_ref, br_ref, tri_ref, h2_ref, slab_ref, cnt_ref)


def _attn_layer(x, cqt, ksh, ckvt, cost, sint, wabst, wqrt, wvt, wo, gffn, wr, br, tri):
    batch, seq, _ = x.shape
    tq = min(SEQ_TILE, seq)
    out_shapes, out_specs = _tail_out(batch, seq, tq)
    return pl.pallas_call(
        _attn_kernel,
        grid=(batch, seq // tq),
        in_specs=[
            pl.BlockSpec((None, tq, D_MODEL), lambda b, i: (b, i, 0)),
            pl.BlockSpec((None, Q_LORA, tq), lambda b, i: (b, 0, i)),
            pl.BlockSpec((None, seq, KEY_WIDTH), lambda b, i: (b, 0, 0)),
            pl.BlockSpec((None, VT_ROWS, seq), lambda b, i: (b, 0, 0)),
            pl.BlockSpec((QK_ROPE // 2, tq), lambda b, i: (0, i)),
            pl.BlockSpec((QK_ROPE // 2, tq), lambda b, i: (0, i)),
            _const_spec((N_HEADS, KV_LORA, Q_LORA)),
            _const_spec((N_HEADS, QK_ROPE, Q_LORA)),
            _const_spec((N_HEADS, V_HEAD, KV_LORA)),
            _const_spec((N_HEADS * V_HEAD, D_MODEL)),
            _const_spec((1, D_MODEL)),
            _const_spec((D_MODEL, LANES)),
            _const_spec((1, LANES)),
            _const_spec((tq, tq)),
        ],
        out_specs=out_specs,
        out_shape=out_shapes,
        scratch_shapes=[pltpu.VMEM((N_HEADS * V_HEAD, tq), BF16),
                        pltpu.VMEM((seq, tq), F32), pltpu.VMEM((seq, tq), F32),
                        pltpu.VMEM((KEY_WIDTH, tq), BF16), pltpu.VMEM((KEY_WIDTH, tq), BF16)],
        compiler_params=pltpu.CompilerParams(
            dimension_semantics=("arbitrary", "arbitrary"), vmem_limit_bytes=VMEM_LIMIT),
        name="mla_attention",
    )(x, cqt, ksh, ckvt, cost, sint, wabst, wqrt, wvt, wo, gffn, wr, br, tri)


def _token_copy(src, src_tok, dst, dst_tok, sem):
    s = pl.multiple_of(src_tok * PACK_ROWS, PACK_ROWS)
    d = pl.multiple_of(dst_tok * PACK_ROWS, PACK_ROWS)
    return pltpu.make_async_copy(src.at[pl.ds(s, PACK_ROWS), :], dst.at[pl.ds(d, PACK_ROWS), :], sem)


def _for_each_token(ts, fn):
    def body(it, c):
        for u in range(ROW_UNROLL):
            fn(it * ROW_UNROLL + u)
        return c
    lax.fori_loop(0, ts // ROW_UNROLL, body, 0)


def _scatter_kernel(zb_ref, zv_ref, h2_hbm, dest_hbm, xs_hbm, idx_smem, stage, zeros_vmem,
                    sem_idx, sem_stage, sem_zero, sem_rows):
    i = pl.program_id(0)
    n_tiles = pl.num_programs(0)
    ts = idx_smem.shape[0] // (2 * IDX_SLOTS)
    tile_rows = ts * PACK_ROWS
    zrows = zeros_vmem.shape[0]
    n_zero = zb_ref.shape[0]

    def zero_copy(e):
        start = pl.multiple_of(zb_ref[e] * PACK_ROWS, zrows)
        return pltpu.make_async_copy(zeros_vmem, xs_hbm.at[pl.ds(start, zrows), :], sem_zero)

    @pl.when(i == 0)
    def _():
        zeros_vmem[...] = jnp.zeros_like(zeros_vmem)

        def start(e, c):
            @pl.when(zv_ref[e] > 0)
            def _():
                zero_copy(e).start()
            return c

        def wait(e, c):
            @pl.when(zv_ref[e] > 0)
            def _():
                zero_copy(e).wait()
            return c

        lax.fori_loop(0, n_zero, start, 0)
        lax.fori_loop(0, n_zero, wait, 0)

    def fetches(t):
        slot = lax.rem(t, IDX_SLOTS)
        rows = h2_hbm.at[pl.ds(pl.multiple_of(t * tile_rows, tile_rows), tile_rows), :]
        return [pltpu.make_async_copy(dest_hbm.at[t], idx_smem.at[pl.ds(slot * (2 * ts), 2 * ts)], sem_idx.at[slot]),
                pltpu.make_async_copy(rows, stage.at[slot], sem_stage.at[slot])]

    def copies(t, r):
        slot = lax.rem(t, IDX_SLOTS)
        sem = sem_rows.at[lax.rem(t, 2)]
        return [_token_copy(stage.at[slot], r, xs_hbm, idx_smem[slot * (2 * ts) + 2 * r + k], sem)
                for k in range(2)]

    def start_tile(t):
        _for_each_token(ts, lambda r: [c.start(priority=k) for k, c in enumerate(copies(t, r))])

    def wait_tile(t):
        _for_each_token(ts, lambda r: [c.wait() for c in copies(t, r)])

    @pl.when(i == 0)
    def _():
        for c in fetches(i):
            c.start()

    for c in fetches(i):
        c.wait()

    @pl.when(i + 1 < n_tiles)
    def _():
        for c in fetches(i + 1):
            c.start()

    start_tile(i)

    @pl.when(i > 0)
    def _():
        wait_tile(i - 1)

    @pl.when(i == n_tiles - 1)
    def _():
        wait_tile(i)


def _scatter_rows(h2, dest, zb, zv, n_rows, bm):
    ts = dest.shape[1] // 2
    tokens = h2.shape[0] // PACK_ROWS
    return pl.pallas_call(
        _scatter_kernel,
        grid_spec=pltpu.PrefetchScalarGridSpec(
            num_scalar_prefetch=2,
            grid=(tokens // ts,),
            in_specs=[pl.BlockSpec(memory_space=pl.ANY),
                      pl.BlockSpec(memory_space=pl.ANY)],
            out_specs=pl.BlockSpec(memory_space=pl.ANY),
            scratch_shapes=[pltpu.SMEM((IDX_SLOTS * 2 * ts,), jnp.int32),
                            pltpu.VMEM((IDX_SLOTS, ts * PACK_ROWS, LANES), jnp.uint32),
                            pltpu.VMEM((bm * PACK_ROWS, LANES), jnp.uint32),
                            pltpu.SemaphoreType.DMA((IDX_SLOTS,)), pltpu.SemaphoreType.DMA((IDX_SLOTS,)),
                            pltpu.SemaphoreType.DMA, pltpu.SemaphoreType.DMA((2,))],
        ),
        out_shape=jax.ShapeDtypeStruct((n_rows * PACK_ROWS, LANES), jnp.uint32),
        compiler_params=pltpu.CompilerParams(
            dimension_semantics=("arbitrary",), vmem_limit_bytes=VMEM_LIMIT),
        name="moe_scatter",
    )(zb, zv, h2, dest)


def _expert_kernel(be_ref, nv_ref, xs_ref, wg_ref, wu_ref, wd_ref, ys_ref, wg_bf, wu_bf, wd_bf):
    b = pl.program_id(0)

    @pl.when(b < nv_ref[0])
    def _():
        changed = (b == 0) | (be_ref[b] != be_ref[jnp.maximum(b - 1, 0)])

        @pl.when(changed)
        def _():
            wg_bf[...] = wg_ref[...].astype(BF16)
            wu_bf[...] = wu_ref[...].astype(BF16)
            wd_bf[...] = wd_ref[...].astype(BF16)

        xb = _from_packed_tiles(xs_ref, xs_ref.shape[0] // PACK_ROWS)
        gate = jnp.dot(xb, wg_bf[...], preferred_element_type=F32)
        up = jnp.dot(xb, wu_bf[...], preferred_element_type=F32)
        hid = gate * (1.0 / (1.0 + jnp.exp(-gate))) * up
        _to_packed_tiles(ys_ref, jnp.dot(hid.astype(BF16), wd_bf[...], preferred_element_type=F32))

    @pl.when(b >= nv_ref[0])
    def _():
        ys_ref[...] = jnp.zeros_like(ys_ref)


def _expert_mlp(xs, block_expert, n_valid, wg, wu, wd, layer, bm):
    n_blocks = xs.shape[0] // (bm * PACK_ROWS)
    blk = lambda b, be, nv: (jnp.minimum(b, nv[0] - 1), 0)
    out_blk = lambda b, be, nv: (b, 0)
    wsel = lambda b, be, nv: (layer, be[jnp.minimum(b, nv[0] - 1)], 0, 0)
    return pl.pallas_call(
        _expert_kernel,
        grid_spec=pltpu.PrefetchScalarGridSpec(
            num_scalar_prefetch=2,
            grid=(n_blocks,),
            in_specs=[pl.BlockSpec((bm * PACK_ROWS, LANES), blk),
                      pl.BlockSpec((None, None, D_MODEL, D_EXPERT), wsel),
                      pl.BlockSpec((None, None, D_MODEL, D_EXPERT), wsel),
                      pl.BlockSpec((None, None, D_EXPERT, D_MODEL), wsel)],
            out_specs=pl.BlockSpec((bm * PACK_ROWS, LANES), out_blk),
            scratch_shapes=[pltpu.VMEM((D_MODEL, D_EXPERT), BF16),
                            pltpu.VMEM((D_MODEL, D_EXPERT), BF16),
                            pltpu.VMEM((D_EXPERT, D_MODEL), BF16)],
        ),
        out_shape=jax.ShapeDtypeStruct(xs.shape, jnp.uint32),
        compiler_params=pltpu.CompilerParams(
            dimension_semantics=("arbitrary",), vmem_limit_bytes=VMEM_LIMIT),
        name="moe_experts",
    )(block_expert, n_valid, xs, wg, wu, wd)


def _combine_kernel(first_tiles, x1_ref, slab_ref, dest_hbm, ys_hbm, *rest):
    if first_tiles is None:
        x2_ref, idx_smem, g0, g1, sem_idx, sem_rows = rest
    else:
        gfin_ref, out_a_ref, out_b_ref, idx_smem, g0, g1, sem_idx, sem_rows = rest
    i = pl.program_id(0)
    n_tiles = pl.num_programs(0)
    ts = x1_ref.shape[0]

    def idx_copy(t):
        slot = lax.rem(t, IDX_SLOTS)
        return pltpu.make_async_copy(dest_hbm.at[t], idx_smem.at[pl.ds(slot * (2 * ts), 2 * ts)], sem_idx.at[slot])

    def copies(t, r):
        slot = lax.rem(t, IDX_SLOTS)
        half = lax.rem(t, 2)
        return [_token_copy(ys_hbm, idx_smem[slot * (2 * ts) + 2 * r + k], g.at[half], r, sem_rows.at[half])
                for k, g in enumerate((g0, g1))]

    def start_tile(t):
        _for_each_token(ts, lambda r: [c.start(priority=k) for k, c in enumerate(copies(t, r))])

    @pl.when(i == 0)
    def _():
        idx_copy(i).start()
        idx_copy(i).wait()
        start_tile(i)

        @pl.when(n_tiles > 1)
        def _():
            idx_copy(i + 1).start()

    @pl.when(i + 1 < n_tiles)
    def _():
        idx_copy(i + 1).wait()
        start_tile(i + 1)

    @pl.when(i + 2 < n_tiles)
    def _():
        idx_copy(i + 2).start()

    _for_each_token(ts, lambda r: [c.wait() for c in copies(i, r)])
    half = lax.rem(i, 2)
    slab = slab_ref[...]
    y = (slab[:, L_W1:L_W1 + 1] * _from_packed_tiles(g0.at[half], ts).astype(F32)
         + slab[:, L_W2:L_W2 + 1] * _from_packed_tiles(g1.at[half], ts).astype(F32))
    x2 = x1_ref[...] + y
    if first_tiles is None:
        x2_ref[...] = x2
    else:
        out = _rms(x2) * gfin_ref[...]

        @pl.when(i < first_tiles)
        def _():
            out_a_ref[...] = out

        @pl.when(i >= first_tiles)
        def _():
            out_b_ref[...] = out


def _combine_rows(x1, slab, dest, ys, final=None):
    tokens = x1.shape[0]
    ts = dest.shape[1] // 2
    tok = lambda i: (i, 0)
    in_specs = [pl.BlockSpec((ts, D_MODEL), tok),
                pl.BlockSpec((ts, LANES), tok),
                pl.BlockSpec(memory_space=pl.ANY),
                pl.BlockSpec(memory_space=pl.ANY)]
    args = [x1, slab, dest, ys]
    if final is None:
        first_tiles = None
        out_specs = pl.BlockSpec((ts, D_MODEL), tok)
        out_shape = jax.ShapeDtypeStruct(x1.shape, F32)
    else:
        gain, first_tokens = final
        first_tiles = first_tokens // ts
        in_specs.append(pl.BlockSpec((1, D_MODEL), lambda i: (0, 0)))
        args.append(gain)
        out_specs = (pl.BlockSpec((ts, D_MODEL), lambda i: (jnp.minimum(i, first_tiles - 1), 0)),
                     pl.BlockSpec((ts, D_MODEL), lambda i: (jnp.maximum(i - first_tiles, 0), 0)))
        out_shape = (jax.ShapeDtypeStruct((first_tokens, D_MODEL), F32),
                     jax.ShapeDtypeStruct((tokens - first_tokens, D_MODEL), F32))
    return pl.pallas_call(
        functools.partial(_combine_kernel, first_tiles),
        grid=(tokens // ts,),
        in_specs=in_specs,
        out_specs=out_specs,
        out_shape=out_shape,
        scratch_shapes=[pltpu.SMEM((IDX_SLOTS * 2 * ts,), jnp.int32),
                        pltpu.VMEM((2, ts * PACK_ROWS, LANES), jnp.uint32),
                        pltpu.VMEM((2, ts * PACK_ROWS, LANES), jnp.uint32),
                        pltpu.SemaphoreType.DMA((IDX_SLOTS,)), pltpu.SemaphoreType.DMA((2,))],
        compiler_params=pltpu.CompilerParams(
            dimension_semantics=("arbitrary",), vmem_limit_bytes=VMEM_LIMIT),
        name="moe_combine",
    )(*args)


def _moe(x1, h2, slab, counts, wg, wu, wd, layer, final=None):
    tokens = x1.shape[0]
    bm = MOE_BLOCK
    ts = min(ROW_TILE, tokens)
    n_blocks = -(-2 * tokens // bm) + N_EXPERTS
    cnt = counts[0, :N_EXPERTS].astype(jnp.int32)
    padded = ((cnt + bm - 1) // bm) * bm
    pend = jnp.cumsum(padded)
    pstart = pend - padded
    eid = slab[:, L_E1:L_E2 + 1].astype(jnp.int32)
    rank = slab[:, L_R1:L_R2 + 1].astype(jnp.int32)
    hit = eid[:, :, None] == jnp.arange(N_EXPERTS, dtype=jnp.int32)
    dest = (jnp.sum(jnp.where(hit, pstart, 0), axis=-1) + rank).reshape(tokens // ts, 2 * ts)
    block_start = jnp.arange(n_blocks, dtype=jnp.int32) * bm
    block_expert = jnp.minimum(jnp.sum(pend[None, :] <= block_start[:, None], axis=1), N_EXPERTS - 1)
    n_valid = (pend[-1:] // bm).astype(jnp.int32)
    trailing = n_valid[0] + jnp.arange(N_EXPERTS, dtype=jnp.int32)
    zb = jnp.concatenate([pend - bm, trailing * bm]).astype(jnp.int32)
    zv = jnp.concatenate([padded > 0, trailing < n_blocks]).astype(jnp.int32)

    xs = _scatter_rows(h2, dest, zb, zv, n_blocks * bm, bm)
    ys = _expert_mlp(xs, block_expert.astype(jnp.int32), n_valid, wg, wu, wd, layer, bm)
    return _combine_rows(x1, slab, dest, ys, final)


def _rope_tables(seq):
    inv_freq = 1.0 / (ROPE_THETA ** (jnp.arange(0, QK_ROPE, 2, dtype=F32) / QK_ROPE))
    ang = jnp.arange(seq, dtype=F32)[:, None] * inv_freq[None, :]
    return jnp.cos(ang), jnp.sin(ang)


def _trunk(inputs, norm_mix, norm_ffn, norm_final, pool_w, pool_scale,
           mla_w_dq, mla_q_norm, mla_w_uq, mla_w_dkv, mla_kv_norm, mla_w_ukv, mla_w_o,
           moe_w_group, moe_b_group, moe_w_router, moe_b_router, moe_w_gate, moe_w_up, moe_w_down):
    seq = inputs[0].shape[1]
    batch = sum(a.shape[0] for a in inputs)
    x = None
    ts = min(SEQ_TILE, seq)
    tri = jnp.tril(jnp.ones((ts, ts), F32), k=-1).astype(BF16)
    cos, sin = _rope_tables(seq)
    lane_pad = ((0, 0), (0, LANES - QK_ROPE))
    cos2 = jnp.pad(jnp.concatenate([cos, cos], axis=1), lane_pad)
    sin2 = jnp.pad(jnp.concatenate([-sin, sin], axis=1), lane_pad)
    cost, sint = cos.T, sin.T
    row = lambda v: v.reshape(1, -1)
    col = lambda v: v.reshape(-1, 1)

    for i in range(DEPTH):
        j = i // 2
        pad = LANES - N_EXPERTS - N_GROUPS
        wr = jnp.pad(jnp.concatenate([moe_w_router[i], moe_w_group[i]], axis=1), ((0, 0), (0, pad))).astype(BF16)
        br = row(jnp.pad(jnp.concatenate([moe_b_router[i], moe_b_group[i]]), (0, pad)))
        gmix, gffn = row(norm_mix[i]), row(norm_ffn[i])
        if i % 2 == 0:
            x1, h2, slab, counts = _pool_layer(inputs if x is None else [x], gmix, pool_w[j].astype(BF16),
                                               row(pool_scale[j]), gffn, wr, br, tri)
        else:
            w_uq = mla_w_uq[j].reshape(Q_LORA, N_HEADS, QK_NOPE + QK_ROPE)
            w_ukv = mla_w_ukv[j].reshape(KV_LORA, N_HEADS, QK_NOPE + V_HEAD)
            wabst = _absorb_weights(w_ukv[:, :, :QK_NOPE].transpose(1, 0, 2),
                                    w_uq[:, :, :QK_NOPE].transpose(1, 2, 0))
            wqrt = w_uq[:, :, QK_NOPE:].transpose(1, 2, 0).astype(BF16)
            wvt = w_ukv[:, :, QK_NOPE:].transpose(1, 2, 0).astype(BF16)
            w_ckv = mla_w_dkv[j][:, :KV_LORA]
            w_kr = mla_w_dkv[j][:, KV_LORA:]
            half = QK_ROPE // 2
            w_kr_swapped = jnp.concatenate([w_kr[:, half:], w_kr[:, :half]], axis=1)
            wkv = jnp.concatenate([w_ckv, jnp.pad(w_kr, lane_pad), jnp.pad(w_kr_swapped, lane_pad)],
                                  axis=1).astype(BF16)
            cqt, ksh, ckvt = _mla_proj(x, gmix, mla_w_dq[j].T.astype(BF16), col(mla_q_norm[j]), wkv,
                                       w_ckv.T.astype(BF16), row(mla_kv_norm[j]), col(mla_kv_norm[j]),
                                       cos2, sin2)
            x1, h2, slab, counts = _attn_layer(x, cqt, ksh, ckvt, cost, sint, wabst, wqrt, wvt,
                                               mla_w_o[j].astype(BF16), gffn, wr, br, tri)
        if i < DEPTH - 1:
            x = _moe(x1, h2, slab, counts, moe_w_gate, moe_w_up, moe_w_down, i).reshape(batch, seq, D_MODEL)
    outs = _moe(x1, h2, slab, counts, moe_w_gate, moe_w_up, moe_w_down, DEPTH - 1,
                final=(row(norm_final), inputs[0].shape[0] * seq))
    return tuple(o.reshape(a.shape) for o, a in zip(outs, inputs))


def kernel(x_prompt, x_sample, norm_mix, norm_ffn, norm_final, pool_w, pool_scale, mla_w_dq, mla_q_norm,
           mla_w_uq, mla_w_dkv, mla_kv_norm, mla_w_ukv, mla_w_o, moe_w_group, moe_b_group, moe_w_router,
           moe_b_router, moe_w_gate, moe_w_up, moe_w_down):
    return _trunk([x_prompt, x_sample], norm_mix, norm_ffn, norm_final, pool_w, pool_scale,
                  mla_w_dq, mla_q_norm, mla_w_uq, mla_w_dkv, mla_kv_norm, mla_w_ukv, mla_w_o,
                  moe_w_group, moe_b_group, moe_w_router, moe_b_router, moe_w_gate, moe_w_up, moe_w_down)
```
